```python
import math
import jax, jax.numpy as jnp
from jax import lax
import numpy as np

D_MODEL = 2048
BATCH = 4
SEQ = 4096
DEPTH = 4

N_A = DEPTH // 2
N_B = DEPTH - N_A

CHUNK = 128
A_WIDTH = D_MODEL
A_GROUPS = 16
A_GROUP_DIM = A_WIDTH // A_GROUPS

N_HEADS = 16
HEAD_DIM = 128
ATT_WIDTH = N_HEADS * HEAD_DIM
Q_BLOCK = 128

FFN_DIM = 5632
CONV_W = 3

EPS = 1e-6

kernel_name = "hybrid_gmlp_fox_yoco_convffn"


def rmsnorm(x, g):
    xf = x.astype(jnp.float32)
    r = lax.rsqrt(jnp.mean(xf * xf, axis=-1, keepdims=True) + EPS)
    return (xf * r).astype(x.dtype) * g


def chunked_gmlp(xn, w_in, v_norm, w_s, b_s, w_out):
    B, S, _ = xn.shape
    z = jax.nn.gelu(xn @ w_in, approximate=False)
    u, v = jnp.split(z, 2, axis=-1)
    v = rmsnorm(v, v_norm)
    v = v.reshape(B, S // CHUNK, CHUNK, A_GROUPS, A_GROUP_DIM)
    causal = jnp.tril(jnp.ones((CHUNK, CHUNK), dtype=bool))
    w = jnp.where(causal[None], w_s, jnp.zeros_like(w_s))
    mixed = jnp.einsum('gts,bnsgc->bntgc', w, v) + b_s.T[:, :, None]
    gated = u * mixed.reshape(B, S, A_WIDTH)
    return gated @ w_out


def conv_ffn(xn, w_up, conv_w, conv_b, w_down):
    S = xn.shape[1]
    h = xn @ w_up
    hp = jnp.pad(h, ((0, 0), (CONV_W - 1, 0), (0, 0)))
    h = sum(hp[:, k:k + S] * conv_w[k] for k in range(CONV_W)) + conv_b
    gate, val = jnp.split(h, 2, axis=-1)
    return (jax.nn.silu(gate) * val) @ w_down


def fox_shared_kv(h, kv_norm, w_kvf, b_f, k_norm):
    B, S, _ = h.shape
    xn = rmsnorm(h, kv_norm)
    kvf = xn @ w_kvf
    k = kvf[..., :ATT_WIDTH].reshape(B, S, N_HEADS, HEAD_DIM)
    v = kvf[..., ATT_WIDTH:2 * ATT_WIDTH].reshape(B, S, N_HEADS, HEAD_DIM)
    f = kvf[..., 2 * ATT_WIDTH:] + b_f
    k = rmsnorm(k, k_norm).transpose(0, 2, 1, 3)
    v = v.transpose(0, 2, 1, 3)
    log_f = jax.nn.log_sigmoid(f.astype(jnp.float32))
    c = jnp.cumsum(log_f, axis=1).transpose(0, 2, 1)
    return k, v, c


def forgetting_attention(xn, w_qg, q_norm, w_out, k, v, c):
    B, S, _ = xn.shape
    qg = xn @ w_qg
    q, g = jnp.split(qg, 2, axis=-1)
    q = rmsnorm(q.reshape(B, S, N_HEADS, HEAD_DIM), q_norm).transpose(0, 2, 1, 3)
    scale = HEAD_DIM ** -0.5
    outs = []
    for i in range(S // Q_BLOCK):
        lo, hi = i * Q_BLOCK, (i + 1) * Q_BLOCK
        qb = q[:, :, lo:hi]
        kb = k[:, :, :hi]
        vb = v[:, :, :hi]
        s = (jnp.einsum('bhqd,bhkd->bhqk', qb, kb).astype(jnp.float32) * scale
             + c[:, :, lo:hi, None] - c[:, :, None, :hi])
        mask = (lo + jnp.arange(Q_BLOCK))[:, None] >= jnp.arange(hi)[None, :]
        s = jnp.where(mask, s, -jnp.inf)
        p = jax.nn.softmax(s, axis=-1).astype(vb.dtype)
        outs.append(jnp.einsum('bhqk,bhkd->bhqd', p, vb))
    o = jnp.concatenate(outs, axis=2).transpose(0, 2, 1, 3).reshape(B, S, ATT_WIDTH)
    o = o * jax.nn.sigmoid(g)
    return o @ w_out


def setup_inputs(seed: int = 0) -> dict:
    key = jax.random.key(seed)
    ks = jax.random.split(key, 24)
    f32 = jnp.float32

    def nrm(k, shape, scale):
        return jax.random.normal(k, shape, f32) * scale

    def gain(k, shape):
        return 1.0 + 0.02 * jax.random.normal(k, shape, f32)

    return {
        "x": jax.random.normal(ks[0], (BATCH, SEQ, D_MODEL), f32),
        "a_norm": gain(ks[1], (N_A, D_MODEL)),
        "a_w_in": nrm(ks[2], (N_A, D_MODEL, 2 * A_WIDTH), D_MODEL ** -0.5),
        "a_v_norm": gain(ks[3], (N_A, A_WIDTH)),
        "a_w_s": nrm(ks[4], (N_A, A_GROUPS, CHUNK, CHUNK), 0.5 * CHUNK ** -0.5),
        "a_b_s": 1.0 + 0.1 * jax.random.normal(ks[5], (N_A, A_GROUPS, CHUNK), f32),
        "a_w_out": nrm(ks[6], (N_A, A_WIDTH, D_MODEL), A_WIDTH ** -0.5),
        "kv_norm": gain(ks[7], (D_MODEL,)),
        "w_kvf": nrm(ks[8], (D_MODEL, 2 * ATT_WIDTH + N_HEADS), D_MODEL ** -0.5),
        "b_f": jax.random.uniform(ks[9], (N_HEADS,), f32, 2.0, 5.0),
        "k_norm": gain(ks[10], (HEAD_DIM,)),
        "b_norm": gain(ks[11], (N_B, D_MODEL)),
        "b_w_qg": nrm(ks[12], (N_B, D_MODEL, 2 * ATT_WIDTH), D_MODEL ** -0.5),
        "q_norm": gain(ks[13], (N_B, HEAD_DIM)),
        "b_w_out": nrm(ks[14], (N_B, ATT_WIDTH, D_MODEL), ATT_WIDTH ** -0.5),
        "f_norm": gain(ks[15], (DEPTH, D_MODEL)),
        "f_w_up": nrm(ks[16], (DEPTH, D_MODEL, 2 * FFN_DIM), D_MODEL ** -0.5),
        "f_conv_w": jnp.zeros((DEPTH, CONV_W, 2 * FFN_DIM), f32).at[:, CONV_W - 1].set(1.0)
                    + nrm(ks[17], (DEPTH, CONV_W, 2 * FFN_DIM), 0.3),
        "f_conv_b": nrm(ks[18], (DEPTH, 2 * FFN_DIM), 0.01),
        "f_w_down": nrm(ks[19], (DEPTH, FFN_DIM, D_MODEL), FFN_DIM ** -0.5),
        "final_norm": gain(ks[20], (D_MODEL,)),
    }


def reference(x, a_norm, a_w_in, a_v_norm, a_w_s, a_b_s, a_w_out,
              kv_norm, w_kvf, b_f, k_norm,
              b_norm, b_w_qg, q_norm, b_w_out,
              f_norm, f_w_up, f_conv_w, f_conv_b, f_w_down, final_norm):
    h = x
    k_sh = v_sh = c_sh = None
    for l in range(DEPTH):
        if l < N_A:
            h = h + chunked_gmlp(rmsnorm(h, a_norm[l]), a_w_in[l], a_v_norm[l],
                                 a_w_s[l], a_b_s[l], a_w_out[l])
        else:
            j = l - N_A
            h = h + forgetting_attention(rmsnorm(h, b_norm[j]), b_w_qg[j], q_norm[j],
                                         b_w_out[j], k_sh, v_sh, c_sh)
        h = h + conv_ffn(rmsnorm(h, f_norm[l]), f_w_up[l], f_conv_w[l],
                         f_conv_b[l], f_w_down[l])
        if l == N_A - 1:
            k_sh, v_sh, c_sh = fox_shared_kv(h, kv_norm, w_kvf, b_f, k_norm)
    return rmsnorm(h, final_norm)
```

```python
import functools
from typing import NamedTuple

import jax
import jax.numpy as jnp
from jax import lax
from jax.experimental import pallas as pl
from jax.experimental.pallas import tpu as pltpu

F32 = jnp.float32
BF16 = jnp.bfloat16
EPS = 1e-6
SUBLANES = 8
LANES = 128
V7X_VMEM_BYTES = 64 * 1024 * 1024


class Tiles(NamedTuple):
    proj_m: int = 1024
    proj_n: int = 512
    gate_m: int = 512
    ffn_m: int = 512
    ffn_n: int = 512
    attn: int = 512


def _params(semantics, vmem_bytes):
    return pltpu.CompilerParams(dimension_semantics=semantics,
                                vmem_limit_bytes=min(vmem_bytes, V7X_VMEM_BYTES - (4 << 20)))


def _rmsnorm(x, gain):
    r = lax.rsqrt(jnp.mean(x * x, axis=-1, keepdims=True) + EPS)
    return (x * r) * gain


def _head_rmsnorm_store(o_ref, x, gain, head_dim, scale=None):
    for i in range(x.shape[1] // head_dim):
        xh = x[:, i * head_dim:(i + 1) * head_dim]
        y = _rmsnorm(xh, gain)
        if scale is not None:
            y = y * scale
        o_ref[:, i * head_dim:(i + 1) * head_dim] = y.astype(o_ref.dtype)


def _gelu(x):
    return 0.5 * x * (1.0 + lax.erf(x * (0.5 ** 0.5)))


def _log_sigmoid(x):
    return jnp.minimum(x, 0.0) - jnp.log1p(jnp.exp(-jnp.abs(x)))


def _dot(a, b):
    return jnp.dot(a, b, preferred_element_type=F32)


def _store_xn(xn_ref, h_ref, g_ref):
    @pl.when(pl.program_id(1) == 0)
    def _():
        xn_ref[...] = _rmsnorm(h_ref[...], g_ref[...]).astype(BF16)


def _a_in_kernel(h_ref, g_ref, wu_ref, wv_ref, u_ref, v_ref, xn_ref):
    _store_xn(xn_ref, h_ref, g_ref)
    xn = xn_ref[...]
    u_ref[...] = _gelu(_dot(xn, wu_ref[...])).astype(BF16)
    v_ref[...] = _gelu(_dot(xn, wv_ref[...])).astype(BF16)


def _a_in(h, gain, w_in, tiles):
    t, d = h.shape
    width = w_in.shape[1] // 2
    tm, tn = min(tiles.proj_m, t), min(tiles.proj_n, width)
    nn = width // tn
    vmem = 2 * tm * d * 4 + tm * d * 2 + 2 * 2 * d * tn * 2 + 2 * 2 * tm * tn * 2 + 6 * tm * tn * 4
    return pl.pallas_call(
        _a_in_kernel,
        grid=(t // tm, nn),
        in_specs=[
            pl.BlockSpec((tm, d), lambda m, n: (m, 0)),
            pl.BlockSpec((1, d), lambda m, n: (0, 0)),
            pl.BlockSpec((d, tn), lambda m, n: (0, n)),
            pl.BlockSpec((d, tn), lambda m, n: (0, n + nn)),
        ],
        out_specs=[pl.BlockSpec((tm, tn), lambda m, n: (m, n))] * 2,
        out_shape=[jax.ShapeDtypeStruct((t, width), BF16)] * 2,
        scratch_shapes=[pltpu.VMEM((tm, d), BF16)],
        compiler_params=_params(("parallel", "arbitrary"), vmem + (8 << 20)),
        name="a_in_proj",
    )(h, gain.reshape(1, d), w_in, w_in)


def _a_gate_kernel(h_ref, u_ref, v_ref, gn_ref, ws_ref, bt_ref, wo_ref, o_ref, gated_ref,
                   *, chunk, groups):
    tm, width = v_ref.shape
    gdim = width // groups
    v = v_ref[...].astype(F32)
    vs = _rmsnorm(v, gn_ref[...]).astype(BF16)
    row = lax.broadcasted_iota(jnp.int32, (chunk, chunk), 0)
    col = lax.broadcasted_iota(jnp.int32, (chunk, chunk), 1)
    causal = row >= col
    for g in range(groups):
        w = jnp.where(causal, ws_ref[g], jnp.zeros((), BF16))
        bias = bt_ref[:, g:g + 1]
        cols = slice(g * gdim, (g + 1) * gdim)
        for c in range(tm // chunk):
            rows = slice(c * chunk, (c + 1) * chunk)
            mixed = _dot(w, vs[rows, cols]) + bias
            gated_ref[rows, cols] = (u_ref[rows, cols].astype(F32) * mixed).astype(BF16)
    o_ref[...] = h_ref[...] + _dot(gated_ref[...], wo_ref[...])


def _a_gate(h, u, v, v_gain, w_s, b_s, w_out, tiles):
    t, d = h.shape
    width = u.shape[1]
    groups, chunk, _ = w_s.shape
    tm = min(tiles.gate_m, t)
    vmem = (4 * tm * d * 4 + 4 * tm * width * 2 + 2 * width * d * 2 + tm * width * 2
            + 3 * tm * width * 4 + tm * d * 4)
    return pl.pallas_call(
        functools.partial(_a_gate_kernel, chunk=chunk, groups=groups),
        grid=(t // tm,),
        in_specs=[
            pl.BlockSpec((tm, d), lambda m: (m, 0)),
            pl.BlockSpec((tm, width), lambda m: (m, 0)),
            pl.BlockSpec((tm, width), lambda m: (m, 0)),
            pl.BlockSpec((1, width), lambda m: (0, 0)),
            pl.BlockSpec((groups, chunk, chunk), lambda m: (0, 0, 0)),
            pl.BlockSpec((chunk, groups), lambda m: (0, 0)),
            pl.BlockSpec((width, d), lambda m: (0, 0), pipeline_mode=pl.Buffered(1)),
        ],
        out_specs=pl.BlockSpec((tm, d), lambda m: (m, 0)),
        out_shape=jax.ShapeDtypeStruct((t, d), F32),
        scratch_shapes=[pltpu.VMEM((tm, width), BF16)],
        compiler_params=_params(("parallel",), vmem + (8 << 20)),
        name="a_gate_out_proj",
    )(h, u, v, v_gain.reshape(1, width), w_s, b_s.T, w_out)


def _ffn_kernel(*refs, tiles_per_seq, final):
    if final:
        (h_ref, g_ref, wg_ref, wv_ref, cwg_ref, cwv_ref, cbg_ref, cbv_ref, wd_ref, fg_ref,
         o_ref, xn_ref, acc_ref, carry_ref, act_ref) = refs
    else:
        (h_ref, g_ref, wg_ref, wv_ref, cwg_ref, cwv_ref, cbg_ref, cbv_ref, wd_ref,
         o_ref, xn_ref, acc_ref, carry_ref, act_ref) = refs
    m, j, nj = pl.program_id(0), pl.program_id(1), pl.num_programs(1)
    tm = h_ref.shape[0]
    _store_xn(xn_ref, h_ref, g_ref)
    xn = xn_ref[...]
    seq_start = (m % tiles_per_seq) == 0
    head_row = lax.broadcasted_iota(jnp.int32, (SUBLANES, wg_ref.shape[1]), 0)

    def conv(w_ref, cw_ref, cb_ref, slot):
        up = _dot(xn, w_ref[...])

        @pl.when(seq_start)
        def _():
            carry_ref[slot] = jnp.zeros(carry_ref.shape[1:], F32)

        prev = carry_ref[slot]
        carry_ref[slot] = up[tm - SUBLANES:, :]
        cw = cw_ref[...]
        w0, w1, w2, b = cw[0:1], cw[1:2], cw[2:3], cb_ref[...]
        body = up * w2 + pltpu.roll(up, 1, axis=0) * w1 + pltpu.roll(up, 2, axis=0) * w0 + b
        top = up[:SUBLANES]
        s1 = jnp.where(head_row == 0, prev[7:8], pltpu.roll(top, 1, axis=0))
        s2 = jnp.where(head_row == 0, prev[6:7],
                       jnp.where(head_row == 1, prev[7:8], pltpu.roll(top, 2, axis=0)))
        head = top * w2 + s1 * w1 + s2 * w0 + b
        return body, head

    gate, gate_head = conv(wg_ref, cwg_ref, cbg_ref, 2 * j)
    val, val_head = conv(wv_ref, cwv_ref, cbv_ref, 2 * j + 1)
    act_ref[...] = (jax.nn.silu(gate) * val).astype(BF16)
    act_ref[:SUBLANES, :] = (jax.nn.silu(gate_head) * val_head).astype(BF16)
    contrib = _dot(act_ref[...], wd_ref[...])

    @pl.when(j == 0)
    def _():
        acc_ref[...] = contrib

    @pl.when(jnp.logical_and(j > 0, j < nj - 1))
    def _():
        acc_ref[...] += contrib

    @pl.when(j == nj - 1)
    def _():
        y = h_ref[...] + (acc_ref[...] + contrib)
        if final:
            y = _rmsnorm(y, fg_ref[...])
        o_ref[...] = y


def _ffn(h, gain, w_up, conv_w, conv_b, w_down, seq_len, tiles, final_gain=None):
    t, d = h.shape
    ffn = w_down.shape[0]
    tm, tn = min(tiles.ffn_m, seq_len), min(tiles.ffn_n, ffn)
    nj = ffn // tn
    assert nj >= 2 and seq_len % tm == 0 and ffn % tn == 0
    final = final_gain is not None
    row = lambda m, j: (m, 0)
    const = lambda m, j: (0, 0)
    in_specs = [
        pl.BlockSpec((tm, d), row),
        pl.BlockSpec((1, d), const),
        pl.BlockSpec((d, tn), lambda m, j: (0, j)),
        pl.BlockSpec((d, tn), lambda m, j: (0, j + nj)),
        pl.BlockSpec((conv_w.shape[0], tn), lambda m, j: (0, j)),
        pl.BlockSpec((conv_w.shape[0], tn), lambda m, j: (0, j + nj)),
        pl.BlockSpec((1, tn), lambda m, j: (0, j)),
        pl.BlockSpec((1, tn), lambda m, j: (0, j + nj)),
        pl.BlockSpec((tn, d), lambda m, j: (j, 0)),
    ]
    conv_b = conv_b.reshape(1, -1)
    args = [h, gain.reshape(1, d), w_up, w_up, conv_w, conv_w, conv_b, conv_b, w_down]
    if final:
        in_specs.append(pl.BlockSpec((1, d), const))
        args.append(final_gain.reshape(1, d))
    vmem = (4 * tm * d * 4 + tm * d * 2 + tm * d * 4 + 2 * 3 * d * tn * 2 + tm * tn * 2
            + 10 * tm * tn * 4 + tm * d * 4)
    return pl.pallas_call(
        functools.partial(_ffn_kernel, tiles_per_seq=seq_len // tm, final=final),
        grid=(t // tm, nj),
        in_specs=in_specs,
        out_specs=pl.BlockSpec((tm, d), row),
        out_shape=jax.ShapeDtypeStruct((t, d), F32),
        scratch_shapes=[
            pltpu.VMEM((tm, d), BF16),
            pltpu.VMEM((tm, d), F32),
            pltpu.VMEM((2 * nj, SUBLANES, tn), F32),
            pltpu.VMEM((tm, tn), BF16),
        ],
        compiler_params=_params(("arbitrary", "arbitrary"), vmem + (8 << 20)),
        name="conv_ffn",
    )(*args)


def _kv_kernel(h_ref, g_ref, wk_ref, wv_ref, wf_ref, kn_ref, k_ref, v_ref, f_ref, xn_ref,
               *, head_dim):
    _store_xn(xn_ref, h_ref, g_ref)
    xn = xn_ref[...]
    _head_rmsnorm_store(k_ref, _dot(xn, wk_ref[...]), kn_ref[...], head_dim)
    v_ref[...] = _dot(xn, wv_ref[...]).astype(BF16)

    @pl.when(pl.program_id(1) == 0)
    def _():
        f_ref[...] = _dot(xn, wf_ref[...])


def _kv_proj(h, gain, w_k, w_v, w_f, k_gain, tiles):
    t, d = h.shape
    width = w_k.shape[1]
    head_dim = k_gain.shape[0]
    tm, tn = min(tiles.proj_m, t), min(tiles.proj_n, width)
    vmem = (2 * tm * d * 4 + tm * d * 2 + 2 * 2 * d * tn * 2 + 2 * d * LANES * 2
            + 2 * 2 * tm * tn * 2 + 2 * tm * LANES * 4 + 6 * tm * tn * 4)
    return pl.pallas_call(
        functools.partial(_kv_kernel, head_dim=head_dim),
        grid=(t // tm, width // tn),
        in_specs=[
            pl.BlockSpec((tm, d), lambda m, n: (m, 0)),
            pl.BlockSpec((1, d), lambda m, n: (0, 0)),
            pl.BlockSpec((d, tn), lambda m, n: (0, n)),
            pl.BlockSpec((d, tn), lambda m, n: (0, n)),
            pl.BlockSpec((d, LANES), lambda m, n: (0, 0)),
            pl.BlockSpec((1, head_dim), lambda m, n: (0, 0)),
        ],
        out_specs=[
            pl.BlockSpec((tm, tn), lambda m, n: (m, n)),
            pl.BlockSpec((tm, tn), lambda m, n: (m, n)),
            pl.BlockSpec((tm, LANES), lambda m, n: (m, 0)),
        ],
        out_shape=[
            jax.ShapeDtypeStruct((t, width), BF16),
            jax.ShapeDtypeStruct((t, width), BF16),
            jax.ShapeDtypeStruct((t, LANES), F32),
        ],
        scratch_shapes=[pltpu.VMEM((tm, d), BF16)],
        compiler_params=_params(("parallel", "arbitrary"), vmem + (8 << 20)),
        name="kv_proj",
    )(h, gain.reshape(1, d), w_k, w_v, w_f, k_gain.reshape(1, head_dim))


def _forget_cumsum_kernel(f_ref, b_ref, c_ref):
    rows, seq = f_ref.shape
    lane = lax.broadcasted_iota(jnp.int32, (rows, LANES), 1)
    carry = jnp.zeros((rows, 1), F32)
    for blk in range(seq // LANES):
        cols = slice(blk * LANES, (blk + 1) * LANES)
        x = _log_sigmoid(f_ref[:, cols] + b_ref[...])
        d = 1
        while d < LANES:
            x = x + jnp.where(lane >= d, pltpu.roll(x, d, axis=1), 0.0)
            d *= 2
        x = x + carry
        c_ref[:, cols] = x
        carry = x[:, LANES - 1:LANES]


def _forget_cumsum(f_rows, b_rows):
    rows, seq = f_rows.shape
    return pl.pallas_call(
        _forget_cumsum_kernel,
        out_shape=jax.ShapeDtypeStruct((rows, seq), F32),
        name="forget_cumsum",
    )(f_rows, b_rows)


def _qg_kernel(h_ref, g_ref, wq_ref, wg_ref, qn_ref, q_ref, sg_ref, xn_ref, *, head_dim):
    _store_xn(xn_ref, h_ref, g_ref)
    xn = xn_ref[...]
    _head_rmsnorm_store(q_ref, _dot(xn, wq_ref[...]), qn_ref[...], head_dim,
                        scale=head_dim ** -0.5)
    sg_ref[...] = jax.nn.sigmoid(_dot(xn, wg_ref[...])).astype(BF16)


def _qg_proj(h, gain, w_qg, q_gain, tiles):
    t, d = h.shape
    width = w_qg.shape[1] // 2
    head_dim = q_gain.shape[0]
    tm, tn = min(tiles.proj_m, t), min(tiles.proj_n, width)
    nn = width // tn
    vmem = 2 * tm * d * 4 + tm * d * 2 + 2 * 2 * d * tn * 2 + 2 * 2 * tm * tn * 2 + 6 * tm * tn * 4
    return pl.pallas_call(
        functools.partial(_qg_kernel, head_dim=head_dim),
        grid=(t // tm, nn),
        in_specs=[
            pl.BlockSpec((tm, d), lambda m, n: (m, 0)),
            pl.BlockSpec((1, d), lambda m, n: (0, 0)),
            pl.BlockSpec((d, tn), lambda m, n: (0, n)),
            pl.BlockSpec((d, tn), lambda m, n: (0, n + nn)),
            pl.BlockSpec((1, head_dim), lambda m, n: (0, 0)),
        ],
        out_specs=[pl.BlockSpec((tm, tn), lambda m, n: (m, n))] * 2,
        out_shape=[jax.ShapeDtypeStruct((t, width), BF16)] * 2,
        scratch_shapes=[pltpu.VMEM((tm, d), BF16)],
        compiler_params=_params(("parallel", "arbitrary"), vmem + (8 << 20)),
        name="qg_proj",
    )(h, gain.reshape(1, d), w_qg, w_qg, q_gain.reshape(1, head_dim))


def _attn_kernel(q_ref, k_ref, v_ref, crow_ref, ccol_ref, sg_ref, o_ref, m_ref, l_ref, acc_ref):
    blk = q_ref.shape[0]
    qi = pl.program_id(2)
    q = q_ref[...]
    c_t = ccol_ref[...]
    m_ref[...] = jnp.full(m_ref.shape, -jnp.inf, F32)
    l_ref[...] = jnp.zeros(l_ref.shape, F32)
    acc_ref[...] = jnp.zeros(acc_ref.shape, F32)

    def block(ki, diagonal):
        start = pl.multiple_of(ki * blk, blk)
        kb = k_ref[pl.ds(start, blk), :]
        vb = v_ref[pl.ds(start, blk), :]
        s = lax.dot_general(q, kb, (((1,), (1,)), ((), ())), preferred_element_type=F32)
        u = s - crow_ref[ki]
        if diagonal:
            row = lax.broadcasted_iota(jnp.int32, (blk, blk), 0)
            col = lax.broadcasted_iota(jnp.int32, (blk, blk), 1)
            u = jnp.where(row >= col, u, -jnp.inf)
        m_old = m_ref[...]
        m_new = jnp.maximum(m_old, c_t + jnp.max(u, axis=1, keepdims=True))
        p = jnp.exp(u + (c_t - m_new))
        alpha = jnp.exp(m_old - m_new)
        l_ref[...] = alpha * l_ref[...] + jnp.sum(p, axis=1, keepdims=True)
        acc_ref[...] = alpha * acc_ref[...] + _dot(p.astype(BF16), vb)
        m_ref[...] = m_new

    def body(ki, carry):
        block(ki, False)
        return carry

    lax.fori_loop(0, qi, body, 0)
    block(qi, True)
    o = acc_ref[...] / l_ref[...]
    o_ref[...] = (o * sg_ref[...].astype(F32)).astype(o_ref.dtype)


def _attention(q, k, v, c, sg, batch, heads, tiles):
    t, width = q.shape
    seq = t // batch
    head_dim = width // heads
    blk = min(tiles.attn, seq)
    nq = seq // blk
    c_row = c.reshape(batch * heads, nq, 1, blk)
    c_col = c.reshape(batch * heads, seq, 1)
    qmap = lambda b, h, i: (b * nq + i, h)
    vmem = (2 * 2 * seq * head_dim * 2 + 2 * 3 * blk * head_dim * 2 + 2 * seq * 4 * SUBLANES
            + 4 * blk * LANES * 4 + blk * head_dim * 4 + 8 * blk * blk * 4)
    return pl.pallas_call(
        _attn_kernel,
        grid=(batch, heads, nq),
        in_specs=[
            pl.BlockSpec((blk, head_dim), qmap),
            pl.BlockSpec((seq, head_dim), lambda b, h, i: (b, h)),
            pl.BlockSpec((seq, head_dim), lambda b, h, i: (b, h)),
            pl.BlockSpec((None, nq, 1, blk), lambda b, h, i: (b * heads + h, 0, 0, 0)),
            pl.BlockSpec((None, blk, 1), lambda b, h, i: (b * heads + h, i, 0)),
            pl.BlockSpec((blk, head_dim), qmap),
        ],
        out_specs=pl.BlockSpec((blk, head_dim), qmap),
        out_shape=jax.ShapeDtypeStruct((t, width), BF16),
        scratch_shapes=[
            pltpu.VMEM((blk, 1), F32),
            pltpu.VMEM((blk, 1), F32),
            pltpu.VMEM((blk, head_dim), F32),
        ],
        compiler_params=_params(("parallel", "parallel", "arbitrary"), vmem + (8 << 20)),
        name="fox_attention",
    )(q, k, v, c_row, c_col, sg)


def _out_proj_kernel(h_ref, a_ref, w_ref, o_ref):
    o_ref[...] = h_ref[...] + _dot(a_ref[...], w_ref[...])


def _out_proj(h, a, w, tiles):
    t, d = h.shape
    kdim = a.shape[1]
    tm, tn = min(tiles.proj_m, t), min(tiles.proj_n, d)
    vmem = 4 * tm * tn * 4 + 2 * tm * kdim * 2 + 2 * kdim * tn * 2 + 2 * tm * tn * 4
    return pl.pallas_call(
        _out_proj_kernel,
        grid=(t // tm, d // tn),
        in_specs=[
            pl.BlockSpec((tm, tn), lambda m, n: (m, n)),
            pl.BlockSpec((tm, kdim), lambda m, n: (m, 0)),
            pl.BlockSpec((kdim, tn), lambda m, n: (0, n)),
        ],
        out_specs=pl.BlockSpec((tm, tn), lambda m, n: (m, n)),
        out_shape=jax.ShapeDtypeStruct((t, d), F32),
        compiler_params=_params(("parallel", "arbitrary"), vmem + (8 << 20)),
        name="b_out_proj",
    )(h, a, w)


def _forward(x, a_norm, a_w_in, a_v_norm, a_w_s, a_b_s, a_w_out, kv_norm, w_kvf, b_f, k_norm,
             b_norm, b_w_qg, q_norm, b_w_out, f_norm, f_w_up, f_conv_w, f_conv_b, f_w_down,
             final_norm, tiles=Tiles()):
    batch, seq, d = x.shape
    depth = f_norm.shape[0]
    n_a = a_norm.shape[0]
    heads = b_f.shape[0]
    att = heads * k_norm.shape[0]
    h = x.reshape(batch * seq, d)
    bf = lambda w: w.astype(BF16)
    k = v = c = None
    for l in range(depth):
        if l < n_a:
            u, vv = _a_in(h, a_norm[l], bf(a_w_in[l]), tiles)
            h = _a_gate(h, u, vv, a_v_norm[l], bf(a_w_s[l]), a_b_s[l], bf(a_w_out[l]), tiles)
        else:
            j = l - n_a
            q, sg = _qg_proj(h, b_norm[j], bf(b_w_qg[j]), q_norm[j], tiles)
            og = _attention(q, k, v, c, sg, batch, heads, tiles)
            h = _out_proj(h, og, bf(b_w_out[j]), tiles)
        h = _ffn(h, f_norm[l], bf(f_w_up[l]), f_conv_w[l], f_conv_b[l], bf(f_w_down[l]), seq,
                 tiles, final_gain=final_norm if l == depth - 1 else None)
        if l == n_a - 1:
            w_f = jnp.pad(w_kvf[:, 2 * att:], ((0, 0), (0, LANES - heads)))
            k, v, f = _kv_proj(h, kv_norm, bf(w_kvf[:, :att]), bf(w_kvf[:, att:2 * att]),
                               bf(w_f), k_norm, tiles)
            f_rows = f[:, :heads].reshape(batch, seq, heads).transpose(0, 2, 1)
            c = _forget_cumsum(f_rows.reshape(batch * heads, seq),
                               jnp.tile(b_f, batch).reshape(batch * heads, 1))
    return h.reshape(batch, seq, d)


def kernel(x, a_norm, a_w_in, a_v_norm, a_w_s, a_b_s, a_w_out, kv_norm, w_kvf, b_f, k_norm,
           b_norm, b_w_qg, q_norm, b_w_out, f_norm, f_w_up, f_conv_w, f_conv_b, f_w_down,
           final_norm):
    return _forward(x, a_norm, a_w_in, a_v_norm, a_w_s, a_b_s, a_w_out, kv_norm, w_kvf, b_f,
                    k_norm, b_norm, b_w_qg, q_norm, b_w_out, f_norm, f_w_up, f_conv_w, f_conv_b,
                    f_w_down, final_norm)
```

```python
import functools
from typing import NamedTuple

import jax
import jax.numpy as jnp
from jax import lax
from jax.experimental import pallas as pl
from jax.experimental.pallas import tpu as pltpu

F32 = jnp.float32
BF16 = jnp.bfloat16
EPS = 1e-6
LOG2E = 1.4426950408889634
SUBLANES = 8
LANES = 128
MXU_COLS = 256
V7X_VMEM_BYTES = 64 * 1024 * 1024


class Tiles(NamedTuple):
    proj_m: int = 1024
    proj_n: int = 512
    gate_m: int = 512
    ffn_m: int = 512
    ffn_n: int = 512
    attn: int = 512
    attn_heads: int = 4


def _params(semantics, vmem_bytes):
    return pltpu.CompilerParams(dimension_semantics=semantics,
                                vmem_limit_bytes=min(vmem_bytes, V7X_VMEM_BYTES - (4 << 20)))


def _rmsnorm(x, gain):
    r = lax.rsqrt(jnp.mean(x * x, axis=-1, keepdims=True) + EPS)
    return (x * r) * gain


def _head_rmsnorm_store(o_ref, x, gain, head_dim, scale=None):
    for i in range(x.shape[1] // head_dim):
        xh = x[:, i * head_dim:(i + 1) * head_dim]
        y = _rmsnorm(xh, gain)
        if scale is not None:
            y = y * scale
        o_ref[:, i * head_dim:(i + 1) * head_dim] = y.astype(o_ref.dtype)


def _gelu(x):
    return 0.5 * x * (1.0 + lax.erf(x * (0.5 ** 0.5)))


def _log_sigmoid(x):
    return jnp.minimum(x, 0.0) - jnp.log1p(jnp.exp(-jnp.abs(x)))


def _dot(a, b):
    return jnp.dot(a, b, preferred_element_type=F32)


def _store_xn(xn_ref, h_ref, g_ref):
    @pl.when(pl.program_id(1) == 0)
    def _():
        xn_ref[...] = _rmsnorm(h_ref[...], g_ref[...]).astype(BF16)


def _a_in_kernel(h_ref, g_ref, wu_ref, wv_ref, u_ref, v_ref, xn_ref):
    _store_xn(xn_ref, h_ref, g_ref)
    xn = xn_ref[...]
    u_ref[...] = _gelu(_dot(xn, wu_ref[...])).astype(BF16)
    v_ref[...] = _gelu(_dot(xn, wv_ref[...])).astype(BF16)


def _a_in(h, gain, w_in, tiles):
    t, d = h.shape
    width = w_in.shape[1] // 2
    tm, tn = min(tiles.proj_m, t), min(tiles.proj_n, width)
    nn = width // tn
    vmem = 2 * tm * d * 4 + tm * d * 2 + 2 * 2 * d * tn * 2 + 2 * 2 * tm * tn * 2 + 6 * tm * tn * 4
    return pl.pallas_call(
        _a_in_kernel,
        grid=(t // tm, nn),
        in_specs=[
            pl.BlockSpec((tm, d), lambda m, n: (m, 0)),
            pl.BlockSpec((1, d), lambda m, n: (0, 0)),
            pl.BlockSpec((d, tn), lambda m, n: (0, n)),
            pl.BlockSpec((d, tn), lambda m, n: (0, n + nn)),
        ],
        out_specs=[pl.BlockSpec((tm, tn), lambda m, n: (m, n))] * 2,
        out_shape=[jax.ShapeDtypeStruct((t, width), BF16)] * 2,
        scratch_shapes=[pltpu.VMEM((tm, d), BF16)],
        compiler_params=_params(("parallel", "arbitrary"), vmem + (8 << 20)),
        name="a_in_proj",
    )(h, gain.reshape(1, d), w_in, w_in)


def _a_gate_kernel(h_ref, u_ref, v_ref, gn_ref, ws_ref, bt_ref, wo_ref, o_ref, gated_ref,
                   *, chunk, groups):
    tm, width = v_ref.shape
    gdim = width // groups
    v = v_ref[...].astype(F32)
    vs = _rmsnorm(v, gn_ref[...]).astype(BF16)
    row = lax.broadcasted_iota(jnp.int32, (chunk, chunk), 0)
    col = lax.broadcasted_iota(jnp.int32, (chunk, chunk), 1)
    causal = row >= col
    for g in range(groups):
        w = jnp.where(causal, ws_ref[g], jnp.zeros((), BF16))
        bias = bt_ref[:, g:g + 1]
        cols = slice(g * gdim, (g + 1) * gdim)
        for c in range(tm // chunk):
            rows = slice(c * chunk, (c + 1) * chunk)
            mixed = _dot(w, vs[rows, cols]) + bias
            gated_ref[rows, cols] = (u_ref[rows, cols].astype(F32) * mixed).astype(BF16)
    o_ref[...] = h_ref[...] + _dot(gated_ref[...], wo_ref[...])


def _a_gate(h, u, v, v_gain, w_s, b_s, w_out, tiles):
    t, d = h.shape
    width = u.shape[1]
    groups, chunk, _ = w_s.shape
    tm = min(tiles.gate_m, t)
    vmem = (4 * tm * d * 4 + 4 * tm * width * 2 + 2 * width * d * 2 + tm * width * 2
            + 3 * tm * width * 4 + tm * d * 4)
    return pl.pallas_call(
        functools.partial(_a_gate_kernel, chunk=chunk, groups=groups),
        grid=(t // tm,),
        in_specs=[
            pl.BlockSpec((tm, d), lambda m: (m, 0)),
            pl.BlockSpec((tm, width), lambda m: (m, 0)),
            pl.BlockSpec((tm, width), lambda m: (m, 0)),
            pl.BlockSpec((1, width), lambda m: (0, 0)),
            pl.BlockSpec((groups, chunk, chunk), lambda m: (0, 0, 0)),
            pl.BlockSpec((chunk, groups), lambda m: (0, 0)),
            pl.BlockSpec((width, d), lambda m: (0, 0), pipeline_mode=pl.Buffered(1)),
        ],
        out_specs=pl.BlockSpec((tm, d), lambda m: (m, 0)),
        out_shape=jax.ShapeDtypeStruct((t, d), F32),
        scratch_shapes=[pltpu.VMEM((tm, width), BF16)],
        compiler_params=_params(("parallel",), vmem + (8 << 20)),
        name="a_gate_out_proj",
    )(h, u, v, v_gain.reshape(1, width), w_s, b_s.T, w_out)


def _ffn_kernel(*refs, tiles_per_seq, final):
    if final:
        (h_ref, g_ref, wg_ref, wv_ref, cwg_ref, cwv_ref, cbg_ref, cbv_ref, wd_ref, fg_ref,
         o_ref, xn_ref, acc_ref, carry_ref, act_ref) = refs
    else:
        (h_ref, g_ref, wg_ref, wv_ref, cwg_ref, cwv_ref, cbg_ref, cbv_ref, wd_ref,
         o_ref, xn_ref, acc_ref, carry_ref, act_ref) = refs
    m, j, nj = pl.program_id(0), pl.program_id(1), pl.num_programs(1) - 1
    tm, d = h_ref.shape
    tn = wg_ref.shape[1]
    sub = min(2 * MXU_COLS, tn)
    seq_start = (m % tiles_per_seq) == 0
    head_row = lax.broadcasted_iota(jnp.int32, (SUBLANES, sub), 0)

    @pl.when(j == 0)
    def _():
        xn_ref[...] = _rmsnorm(h_ref[...], g_ref[...]).astype(BF16)
        acc_ref[...] = jnp.zeros(acc_ref.shape, F32)

    @pl.when(jnp.logical_and(seq_start, j < nj))
    def _():
        carry_ref[2 * j] = jnp.zeros(carry_ref.shape[1:], F32)
        carry_ref[2 * j + 1] = jnp.zeros(carry_ref.shape[1:], F32)

    def conv(w_ref, cw_ref, cb_ref, slot, cols):
        up = _dot(xn_ref[...], w_ref[:, cols])
        prev = carry_ref[slot, :, cols]
        carry_ref[slot, :, cols] = up[tm - SUBLANES:, :]
        w0, w1, w2, b = cw_ref[0:1, cols], cw_ref[1:2, cols], cw_ref[2:3, cols], cb_ref[:, cols]
        body = up * w2 + pltpu.roll(up, 1, axis=0) * w1 + pltpu.roll(up, 2, axis=0) * w0 + b
        top = up[:SUBLANES]
        s1 = jnp.where(head_row == 0, prev[7:8], pltpu.roll(top, 1, axis=0))
        s2 = jnp.where(head_row == 0, prev[6:7],
                       jnp.where(head_row == 1, prev[7:8], pltpu.roll(top, 2, axis=0)))
        head = top * w2 + s1 * w1 + s2 * w0 + b
        return body, head

    def up_proj():
        slot = j % 2
        for c in range(tn // sub):
            cols = slice(c * sub, (c + 1) * sub)
            gate, gate_head = conv(wg_ref, cwg_ref, cbg_ref, 2 * j, cols)
            val, val_head = conv(wv_ref, cwv_ref, cbv_ref, 2 * j + 1, cols)
            act_ref[slot, :, cols] = (jax.nn.silu(gate) * val).astype(BF16)
            act_ref[slot, :SUBLANES, cols] = (jax.nn.silu(gate_head) * val_head).astype(BF16)

    def down_proj():
        act = act_ref[(j + 1) % 2]
        width = min(4 * MXU_COLS, d)
        for c in range(d // width):
            cols = slice(c * width, (c + 1) * width)
            acc_ref[:, cols] += _dot(act, wd_ref[:, cols])

    @pl.when(j == 0)
    def _():
        up_proj()

    @pl.when(jnp.logical_and(j > 0, j < nj))
    def _():
        up_proj()
        down_proj()

    @pl.when(j == nj)
    def _():
        down_proj()
        y = h_ref[...] + acc_ref[...]
        if final:
            y = _rmsnorm(y, fg_ref[...])
        o_ref[...] = y


def _ffn(h, gain, w_up, conv_w, conv_b, w_down, seq_len, tiles, final_gain=None):
    t, d = h.shape
    ffn = w_down.shape[0]
    tm, tn = min(tiles.ffn_m, seq_len), min(tiles.ffn_n, ffn)
    nj = ffn // tn
    assert seq_len % tm == 0 and ffn % tn == 0
    final = final_gain is not None
    row = lambda m, j: (m, 0)
    const = lambda m, j: (0, 0)
    gate_tile = lambda m, j: (0, jnp.minimum(j, nj - 1))
    val_tile = lambda m, j: (0, jnp.minimum(j, nj - 1) + nj)
    in_specs = [
        pl.BlockSpec((tm, d), row),
        pl.BlockSpec((1, d), const),
        pl.BlockSpec((d, tn), gate_tile),
        pl.BlockSpec((d, tn), val_tile),
        pl.BlockSpec((conv_w.shape[0], tn), gate_tile),
        pl.BlockSpec((conv_w.shape[0], tn), val_tile),
        pl.BlockSpec((1, tn), gate_tile),
        pl.BlockSpec((1, tn), val_tile),
        pl.BlockSpec((tn, d), lambda m, j: (jnp.maximum(j - 1, 0), 0)),
    ]
    conv_b = conv_b.reshape(1, -1)
    args = [h, gain.reshape(1, d), w_up, w_up, conv_w, conv_w, conv_b, conv_b, w_down]
    if final:
        in_specs.append(pl.BlockSpec((1, d), const))
        args.append(final_gain.reshape(1, d))
    vmem = (4 * tm * d * 4 + tm * d * 2 + tm * d * 4 + 2 * 3 * d * tn * 2 + 2 * tm * tn * 2
            + 10 * tm * tn * 4 + tm * d * 4)
    return pl.pallas_call(
        functools.partial(_ffn_kernel, tiles_per_seq=seq_len // tm, final=final),
        grid=(t // tm, nj + 1),
        in_specs=in_specs,
        out_specs=pl.BlockSpec((tm, d), row),
        out_shape=jax.ShapeDtypeStruct((t, d), F32),
        scratch_shapes=[
            pltpu.VMEM((tm, d), BF16),
            pltpu.VMEM((tm, d), F32),
            pltpu.VMEM((2 * nj, SUBLANES, tn), F32),
            pltpu.VMEM((2, tm, tn), BF16),
        ],
        compiler_params=_params(("arbitrary", "arbitrary"), vmem + (8 << 20)),
        name="conv_ffn",
    )(*args)


def _kv_kernel(h_ref, g_ref, wkt_ref, wv_ref, wf_ref, kn_ref, kt_ref, v_ref, f_ref, xn_ref,
               *, head_dim):
    _store_xn(xn_ref, h_ref, g_ref)
    xn = xn_ref[...]
    kt = lax.dot_general(wkt_ref[...], xn, (((1,), (1,)), ((), ())), preferred_element_type=F32)
    for i in range(kt.shape[0] // head_dim):
        rows = slice(i * head_dim, (i + 1) * head_dim)
        x = kt[rows]
        r = lax.rsqrt(jnp.mean(x * x, axis=0, keepdims=True) + EPS)
        kt_ref[rows, :] = ((x * r) * kn_ref[...]).astype(BF16)
    v_ref[...] = _dot(xn, wv_ref[...]).astype(BF16)

    @pl.when(pl.program_id(1) == 0)
    def _():
        f_ref[...] = _dot(xn, wf_ref[...])


def _kv_proj(h, gain, w_kt, w_v, w_f, k_gain, tiles):
    t, d = h.shape
    width = w_v.shape[1]
    head_dim = k_gain.shape[0]
    tm, tn = min(tiles.proj_m, t), min(tiles.proj_n, width)
    vmem = (2 * tm * d * 4 + tm * d * 2 + 2 * 2 * d * tn * 2 + 2 * d * LANES * 2
            + 2 * 2 * tm * tn * 2 + 2 * tm * LANES * 4 + 6 * tm * tn * 4)
    return pl.pallas_call(
        functools.partial(_kv_kernel, head_dim=head_dim),
        grid=(t // tm, width // tn),
        in_specs=[
            pl.BlockSpec((tm, d), lambda m, n: (m, 0)),
            pl.BlockSpec((1, d), lambda m, n: (0, 0)),
            pl.BlockSpec((tn, d), lambda m, n: (n, 0)),
            pl.BlockSpec((d, tn), lambda m, n: (0, n)),
            pl.BlockSpec((d, LANES), lambda m, n: (0, 0)),
            pl.BlockSpec((head_dim, 1), lambda m, n: (0, 0)),
        ],
        out_specs=[
            pl.BlockSpec((tn, tm), lambda m, n: (n, m)),
            pl.BlockSpec((tm, tn), lambda m, n: (m, n)),
            pl.BlockSpec((tm, LANES), lambda m, n: (m, 0)),
        ],
        out_shape=[
            jax.ShapeDtypeStruct((width, t), BF16),
            jax.ShapeDtypeStruct((t, width), BF16),
            jax.ShapeDtypeStruct((t, LANES), F32),
        ],
        scratch_shapes=[pltpu.VMEM((tm, d), BF16)],
        compiler_params=_params(("parallel", "arbitrary"), vmem + (8 << 20)),
        name="kv_proj",
    )(h, gain.reshape(1, d), w_kt, w_v, w_f, k_gain.reshape(head_dim, 1))


def _forget_cumsum_kernel(f_ref, b_ref, c_ref):
    rows, seq = f_ref.shape
    lane = lax.broadcasted_iota(jnp.int32, (rows, LANES), 1)
    carry = jnp.zeros((rows, 1), F32)
    for blk in range(seq // LANES):
        cols = slice(blk * LANES, (blk + 1) * LANES)
        x = _log_sigmoid(f_ref[:, cols] + b_ref[...])
        d = 1
        while d < LANES:
            x = x + jnp.where(lane >= d, pltpu.roll(x, d, axis=1), 0.0)
            d *= 2
        x = x + carry
        c_ref[:, cols] = x
        carry = x[:, LANES - 1:LANES]


def _forget_cumsum(f_rows, b_rows):
    rows, seq = f_rows.shape
    return pl.pallas_call(
        _forget_cumsum_kernel,
        out_shape=jax.ShapeDtypeStruct((rows, seq), F32),
        name="forget_cumsum",
    )(f_rows, b_rows)


def _qg_kernel(h_ref, g_ref, wq_ref, wg_ref, qn_ref, q_ref, sg_ref, xn_ref, *, head_dim):
    _store_xn(xn_ref, h_ref, g_ref)
    xn = xn_ref[...]
    _head_rmsnorm_store(q_ref, _dot(xn, wq_ref[...]), qn_ref[...], head_dim,
                        scale=head_dim ** -0.5 * LOG2E)
    sg_ref[...] = jax.nn.sigmoid(_dot(xn, wg_ref[...])).astype(BF16)


def _qg_proj(h, gain, w_qg, q_gain, tiles):
    t, d = h.shape
    width = w_qg.shape[1] // 2
    head_dim = q_gain.shape[0]
    tm, tn = min(tiles.proj_m, t), min(tiles.proj_n, width)
    nn = width // tn
    vmem = 2 * tm * d * 4 + tm * d * 2 + 2 * 2 * d * tn * 2 + 2 * 2 * tm * tn * 2 + 6 * tm * tn * 4
    return pl.pallas_call(
        functools.partial(_qg_kernel, head_dim=head_dim),
        grid=(t // tm, nn),
        in_specs=[
            pl.BlockSpec((tm, d), lambda m, n: (m, 0)),
            pl.BlockSpec((1, d), lambda m, n: (0, 0)),
            pl.BlockSpec((d, tn), lambda m, n: (0, n)),
            pl.BlockSpec((d, tn), lambda m, n: (0, n + nn)),
            pl.BlockSpec((1, head_dim), lambda m, n: (0, 0)),
        ],
        out_specs=[pl.BlockSpec((tm, tn), lambda m, n: (m, n))] * 2,
        out_shape=[jax.ShapeDtypeStruct((t, width), BF16)] * 2,
        scratch_shapes=[pltpu.VMEM((tm, d), BF16)],
        compiler_params=_params(("parallel", "arbitrary"), vmem + (8 << 20)),
        name="qg_proj",
    )(h, gain.reshape(1, d), w_qg, w_qg, q_gain.reshape(1, head_dim))


C_TERMS = 3


def _split_bf16_terms(x):
    terms = []
    for _ in range(C_TERMS):
        t = x.astype(BF16).astype(F32)
        terms.append(t)
        x = x - t
    return terms


def _attn_kernel(q_ref, kt_ref, v_ref, crow_ref, ccol_ref, sg_ref, o_ref,
                 qa_ref, kta_ref, va_ref, m_ref, acc_ref, *, heads, head_dim):
    blk = q_ref.shape[0]
    nk = kta_ref.shape[1]
    hd = head_dim
    qi = pl.program_id(2)

    @pl.when(qi == 0)
    def _():
        sub = lax.broadcasted_iota(jnp.int32, (2 * SUBLANES, blk), 0)
        for h in range(heads):
            for kb in range(nk):
                cols = slice(kb * blk, (kb + 1) * blk)
                c_s = _split_bf16_terms(crow_ref[h, :, cols] * LOG2E)
                tail = jnp.where(sub < C_TERMS, 1.0, 0.0)
                for j, term in enumerate(c_s):
                    tail = jnp.where(sub == C_TERMS + j, -term, tail)
                kta_ref[h, kb, :hd, :] = kt_ref[h * hd:(h + 1) * hd, cols]
                kta_ref[h, kb, hd:hd + 2 * SUBLANES, :] = tail.astype(BF16)
                kta_ref[h, kb, hd + 2 * SUBLANES:, :] = jnp.zeros((hd - 2 * SUBLANES, blk), BF16)
            va_ref[h, :, :hd] = v_ref[:, h * hd:(h + 1) * hd]
            va_ref[h, :, hd:] = jnp.ones((v_ref.shape[0], hd), BF16)

    lane = lax.broadcasted_iota(jnp.int32, (blk, hd), 1)
    for h in range(heads):
        c_t = _split_bf16_terms(ccol_ref[h] * LOG2E)
        tail = jnp.where(jnp.logical_and(lane >= C_TERMS, lane < 2 * C_TERMS), 1.0, 0.0)
        for j, term in enumerate(c_t):
            tail = jnp.where(lane == j, term, tail)
        qa_ref[h, :, :hd] = q_ref[:, h * hd:(h + 1) * hd]
        qa_ref[h, :, hd:] = tail.astype(BF16)
        m_ref[h] = jnp.full((blk, LANES), -jnp.inf, F32)
        acc_ref[h] = jnp.zeros((blk, 2 * hd), F32)

    def block(ki, diagonal):
        start = pl.multiple_of(ki * blk, blk)
        for h in range(heads):
            s = _dot(qa_ref[h], kta_ref[h, ki])
            if diagonal:
                row = lax.broadcasted_iota(jnp.int32, (blk, blk), 0)
                col = lax.broadcasted_iota(jnp.int32, (blk, blk), 1)
                s = jnp.where(row >= col, s, -jnp.inf)
            chunks = [s[:, c * LANES:(c + 1) * LANES] for c in range(blk // LANES)]
            m_old = m_ref[h]
            m_new = jnp.maximum(m_old, jnp.max(functools.reduce(jnp.maximum, chunks),
                                               axis=1, keepdims=True))
            p = jnp.concatenate([jnp.exp2(c - m_new).astype(BF16) for c in chunks], axis=1)
            alpha = jnp.exp2(m_old - m_new)
            pv = _dot(p, va_ref[h, pl.ds(start, blk), :])
            acc_ref[h] = jnp.concatenate([alpha] * (2 * hd // LANES), axis=1) * acc_ref[h] + pv
            m_ref[h] = m_new

    def body(ki, carry):
        block(ki, False)
        return carry

    lax.fori_loop(0, qi, body, 0)
    block(qi, True)
    for h in range(heads):
        acc = acc_ref[h]
        o = acc[:, :hd] / acc[:, hd:]
        cols = slice(h * hd, (h + 1) * hd)
        o_ref[:, cols] = (o * sg_ref[:, cols].astype(F32)).astype(o_ref.dtype)


def _attention(q, kt, v, c, sg, batch, heads, tiles):
    t, width = q.shape
    seq = t // batch
    hd = width // heads
    assert hd == LANES
    blk = min(tiles.attn, seq)
    nq = seq // blk
    hp = min(tiles.attn_heads, heads)
    c_row = c.reshape(batch * heads, 1, seq)
    c_col = c.reshape(batch * heads, seq, 1)
    qmap = lambda b, g, i: (b * nq + i, g)
    vmem = (2 * 2 * seq * hp * hd * 2 + 2 * seq * hp * 2 * hd * 2 + 2 * 3 * blk * hp * hd * 2
            + hp * blk * (2 * hd * 2 + LANES * 4 + 2 * hd * 4) + 2 * hp * seq * 4 * SUBLANES
            + 2 * hp * blk * LANES * 4 + hp * 3 * blk * blk * 4)
    return pl.pallas_call(
        functools.partial(_attn_kernel, heads=hp, head_dim=hd),
        grid=(batch, heads // hp, nq),
        in_specs=[
            pl.BlockSpec((blk, hp * hd), qmap),
            pl.BlockSpec((hp * hd, seq), lambda b, g, i: (g, b)),
            pl.BlockSpec((seq, hp * hd), lambda b, g, i: (b, g)),
            pl.BlockSpec((hp, 1, seq), lambda b, g, i: (b * (heads // hp) + g, 0, 0)),
            pl.BlockSpec((hp, blk, 1), lambda b, g, i: (b * (heads // hp) + g, i, 0)),
            pl.BlockSpec((blk, hp * hd), qmap),
        ],
        out_specs=pl.BlockSpec((blk, hp * hd), qmap),
        out_shape=jax.ShapeDtypeStruct((t, width), BF16),
        scratch_shapes=[
            pltpu.VMEM((hp, blk, 2 * hd), BF16),
            pltpu.VMEM((hp, nq, 2 * hd, blk), BF16),
            pltpu.VMEM((hp, seq, 2 * hd), BF16),
            pltpu.VMEM((hp, blk, LANES), F32),
            pltpu.VMEM((hp, blk, 2 * hd), F32),
        ],
        compiler_params=_params(("parallel", "parallel", "arbitrary"), vmem + (8 << 20)),
        name="fox_attention",
    )(q, kt, v, c_row, c_col, sg)


def _out_proj_kernel(h_ref, a_ref, w_ref, o_ref):
    o_ref[...] = h_ref[...] + _dot(a_ref[...], w_ref[...])


def _out_proj(h, a, w, tiles):
    t, d = h.shape
    kdim = a.shape[1]
    tm, tn = min(tiles.proj_m, t), min(tiles.proj_n, d)
    vmem = 4 * tm * tn * 4 + 2 * tm * kdim * 2 + 2 * kdim * tn * 2 + 2 * tm * tn * 4
    return pl.pallas_call(
        _out_proj_kernel,
        grid=(t // tm, d // tn),
        in_specs=[
            pl.BlockSpec((tm, tn), lambda m, n: (m, n)),
            pl.BlockSpec((tm, kdim), lambda m, n: (m, 0)),
            pl.BlockSpec((kdim, tn), lambda m, n: (0, n)),
        ],
        out_specs=pl.BlockSpec((tm, tn), lambda m, n: (m, n)),
        out_shape=jax.ShapeDtypeStruct((t, d), F32),
        compiler_params=_params(("parallel", "arbitrary"), vmem + (8 << 20)),
        name="b_out_proj",
    )(h, a, w)


def _forward(x, a_norm, a_w_in, a_v_norm, a_w_s, a_b_s, a_w_out, kv_norm, w_kvf, b_f, k_norm,
             b_norm, b_w_qg, q_norm, b_w_out, f_norm, f_w_up, f_conv_w, f_conv_b, f_w_down,
             final_norm, tiles=Tiles()):
    batch, seq, d = x.shape
    depth = f_norm.shape[0]
    n_a = a_norm.shape[0]
    heads = b_f.shape[0]
    att = heads * k_norm.shape[0]
    h = x.reshape(batch * seq, d)
    bf = lambda w: w.astype(BF16)
    k = v = c = None
    for l in range(depth):
        if l < n_a:
            u, vv = _a_in(h, a_norm[l], bf(a_w_in[l]), tiles)
            h = _a_gate(h, u, vv, a_v_norm[l], bf(a_w_s[l]), a_b_s[l], bf(a_w_out[l]), tiles)
        else:
            j = l - n_a
            q, sg = _qg_proj(h, b_norm[j], bf(b_w_qg[j]), q_norm[j], tiles)
            og = _attention(q, k, v, c, sg, batch, heads, tiles)
            h = _out_proj(h, og, bf(b_w_out[j]), tiles)
        h = _ffn(h, f_norm[l], bf(f_w_up[l]), f_conv_w[l], f_conv_b[l], bf(f_w_down[l]), seq,
                 tiles, final_gain=final_norm if l == depth - 1 else None)
        if l == n_a - 1:
            w_f = jnp.pad(w_kvf[:, 2 * att:], ((0, 0), (0, LANES - heads)))
            k, v, f = _kv_proj(h, kv_norm, bf(w_kvf[:, :att]).T, bf(w_kvf[:, att:2 * att]),
                               bf(w_f), k_norm, tiles)
            f_rows = f[:, :heads].reshape(batch, seq, heads).transpose(0, 2, 1)
            c = _forget_cumsum(f_rows.reshape(batch * heads, seq),
                               jnp.tile(b_f, batch).reshape(batch * heads, 1))
    return h.reshape(batch, seq, d)


def kernel(x, a_norm, a_w_in, a_v_norm, a_w_s, a_b_s, a_w_out, kv_norm, w_kvf, b_f, k_norm,
           b_norm, b_w_qg, q_norm, b_w_out, f_norm, f_w_up, f_conv_w, f_conv_b, f_w_down,
           final_norm):
    return _forward(x, a_norm, a_w_in, a_v_norm, a_w_s, a_b_s, a_w_out, kv_norm, w_kvf, b_f,
                    k_norm, b_norm, b_w_qg, q_norm, b_w_out, f_norm, f_w_up, f_conv_w, f_conv_b,
                    f_w_down, final_norm)
```

```python
import functools
from typing import NamedTuple

import jax
import jax.numpy as jnp
from jax import lax
from jax.experimental import pallas as pl
from jax.experimental.pallas import tpu as pltpu

F32 = jnp.float32
BF16 = jnp.bfloat16
EPS = 1e-6
LOG2E = 1.4426950408889634
SUBLANES = 8
LANES = 128
MXU_COLS = 256
V7X_VMEM_BYTES = 64 * 1024 * 1024


class Tiles(NamedTuple):
    proj_m: int = 1024
    proj_n: int = 512
    gate_m: int = 512
    ffn_m: int = 512
    ffn_n: int = 512
    attn: int = 512
    attn_heads: int = 4


def _params(semantics, vmem_bytes):
    return pltpu.CompilerParams(dimension_semantics=semantics,
                                vmem_limit_bytes=min(vmem_bytes, V7X_VMEM_BYTES - (4 << 20)))


def _rmsnorm(x, gain):
    r = lax.rsqrt(jnp.mean(x * x, axis=-1, keepdims=True) + EPS)
    return (x * r) * gain


def _head_rmsnorm_store(o_ref, x, gain, head_dim, scale=None):
    for i in range(x.shape[1] // head_dim):
        xh = x[:, i * head_dim:(i + 1) * head_dim]
        y = _rmsnorm(xh, gain)
        if scale is not None:
            y = y * scale
        o_ref[:, i * head_dim:(i + 1) * head_dim] = y.astype(o_ref.dtype)


def _gelu(x):
    return 0.5 * x * (1.0 + lax.erf(x * (0.5 ** 0.5)))


def _log_sigmoid(x):
    return jnp.minimum(x, 0.0) - jnp.log1p(jnp.exp(-jnp.abs(x)))


def _dot(a, b):
    return jnp.dot(a, b, preferred_element_type=F32)


def _store_xn(xn_ref, h_ref, g_ref):
    @pl.when(pl.program_id(1) == 0)
    def _():
        xn_ref[...] = _rmsnorm(h_ref[...], g_ref[...]).astype(BF16)


def _a_in_kernel(h_ref, g_ref, wu_ref, wv_ref, u_ref, v_ref, xn_ref):
    _store_xn(xn_ref, h_ref, g_ref)
    xn = xn_ref[...]
    u_ref[...] = _gelu(_dot(xn, wu_ref[...])).astype(BF16)
    v_ref[...] = _gelu(_dot(xn, wv_ref[...])).astype(BF16)


def _a_in(h, gain, w_in, layer, tiles):
    t, d = h.shape
    width = w_in.shape[2] // 2
    tm, tn = min(tiles.proj_m, t), min(tiles.proj_n, width)
    nn = width // tn
    vmem = 2 * tm * d * 4 + tm * d * 2 + 2 * 2 * d * tn * 2 + 2 * 2 * tm * tn * 2 + 6 * tm * tn * 4
    return pl.pallas_call(
        _a_in_kernel,
        grid=(t // tm, nn),
        in_specs=[
            pl.BlockSpec((tm, d), lambda m, n: (m, 0)),
            pl.BlockSpec((1, d), lambda m, n: (0, 0)),
            pl.BlockSpec((None, d, tn), lambda m, n: (layer, 0, n)),
            pl.BlockSpec((None, d, tn), lambda m, n: (layer, 0, n + nn)),
        ],
        out_specs=[pl.BlockSpec((tm, tn), lambda m, n: (m, n))] * 2,
        out_shape=[jax.ShapeDtypeStruct((t, width), BF16)] * 2,
        scratch_shapes=[pltpu.VMEM((tm, d), BF16)],
        compiler_params=_params(("parallel", "arbitrary"), vmem + (8 << 20)),
        name="a_in_proj",
    )(h, gain.reshape(1, d), w_in, w_in)


def _a_gate_kernel(h_ref, u_ref, v_ref, gn_ref, ws_ref, bt_ref, wo_ref, o_ref, gated_ref,
                   *, chunk, groups):
    tm, width = v_ref.shape
    gdim = width // groups
    v = v_ref[...].astype(F32)
    vs = _rmsnorm(v, gn_ref[...]).astype(BF16)
    row = lax.broadcasted_iota(jnp.int32, (chunk, chunk), 0)
    col = lax.broadcasted_iota(jnp.int32, (chunk, chunk), 1)
    causal = row >= col
    for g in range(groups):
        w = jnp.where(causal, ws_ref[g], jnp.zeros((), BF16))
        bias = bt_ref[:, g:g + 1]
        cols = slice(g * gdim, (g + 1) * gdim)
        for c in range(tm // chunk):
            rows = slice(c * chunk, (c + 1) * chunk)
            mixed = _dot(w, vs[rows, cols]) + bias
            gated_ref[rows, cols] = (u_ref[rows, cols].astype(F32) * mixed).astype(BF16)
    o_ref[...] = h_ref[...] + _dot(gated_ref[...], wo_ref[...])


def _a_gate(h, u, v, v_gain, w_s, b_s, w_out, layer, tiles):
    t, d = h.shape
    width = u.shape[1]
    _, groups, chunk, _ = w_s.shape
    tm = min(tiles.gate_m, t)
    vmem = (4 * tm * d * 4 + 4 * tm * width * 2 + 2 * width * d * 2 + tm * width * 2
            + 3 * tm * width * 4 + tm * d * 4)
    return pl.pallas_call(
        functools.partial(_a_gate_kernel, chunk=chunk, groups=groups),
        grid=(t // tm,),
        in_specs=[
            pl.BlockSpec((tm, d), lambda m: (m, 0)),
            pl.BlockSpec((tm, width), lambda m: (m, 0)),
            pl.BlockSpec((tm, width), lambda m: (m, 0)),
            pl.BlockSpec((1, width), lambda m: (0, 0)),
            pl.BlockSpec((None, groups, chunk, chunk), lambda m: (layer, 0, 0, 0)),
            pl.BlockSpec((chunk, groups), lambda m: (0, 0)),
            pl.BlockSpec((None, width, d), lambda m: (layer, 0, 0), pipeline_mode=pl.Buffered(1)),
        ],
        out_specs=pl.BlockSpec((tm, d), lambda m: (m, 0)),
        out_shape=jax.ShapeDtypeStruct((t, d), F32),
        scratch_shapes=[pltpu.VMEM((tm, width), BF16)],
        compiler_params=_params(("parallel",), vmem + (8 << 20)),
        name="a_gate_out_proj",
    )(h, u, v, v_gain.reshape(1, width), w_s, b_s.T, w_out)


def _ffn_kernel(*refs, tiles_per_seq, nj, final):
    if final:
        (h_ref, g_ref, wg_ref, wv_ref, cwg_ref, cwv_ref, cbg_ref, cbv_ref, wd_ref, fg_ref,
         o_ref, xn_ref, acc_ref, carry_ref, act0_ref, act1_ref, upg_ref, upv_ref) = refs
    else:
        (h_ref, g_ref, wg_ref, wv_ref, cwg_ref, cwv_ref, cbg_ref, cbv_ref, wd_ref,
         o_ref, xn_ref, acc_ref, carry_ref, act0_ref, act1_ref, upg_ref, upv_ref) = refs
    acts = (act0_ref, act1_ref)
    m, j = pl.program_id(0), pl.program_id(1)
    tm, d = h_ref.shape
    tn = wg_ref.shape[1]
    seq_start = (m % tiles_per_seq) == 0

    @pl.when(j == 0)
    def _():
        xn_ref[...] = _rmsnorm(h_ref[...], g_ref[...]).astype(BF16)
        acc_ref[...] = jnp.zeros(acc_ref.shape, F32)

    @pl.when(jnp.logical_and(seq_start, j < nj))
    def _():
        carry_ref[2 * j] = jnp.zeros(carry_ref.shape[1:], F32)
        carry_ref[2 * j + 1] = jnp.zeros(carry_ref.shape[1:], F32)

    def conv(w_ref, cw_ref, cb_ref, slot, up_ref):
        up_ref[:SUBLANES] = carry_ref[slot]
        up = _dot(xn_ref[...], w_ref[...])
        up_ref[SUBLANES:] = up
        carry_ref[slot] = up[tm - SUBLANES:, :]
        w0, w1, w2, b = cw_ref[0:1], cw_ref[1:2], cw_ref[2:3], cb_ref[...]
        return (up * w2 + up_ref[SUBLANES - 1:SUBLANES - 1 + tm] * w1
                + up_ref[SUBLANES - 2:SUBLANES - 2 + tm] * w0 + b)

    def up_proj(act_ref):
        gate = conv(wg_ref, cwg_ref, cbg_ref, 2 * j, upg_ref)
        val = conv(wv_ref, cwv_ref, cbv_ref, 2 * j + 1, upv_ref)
        act_ref[...] = (jax.nn.silu(gate) * val).astype(BF16)

    def down_proj(act_ref):
        acc_ref[...] += _dot(act_ref[...], wd_ref[...])

    @pl.when(j == 0)
    def _():
        up_proj(acts[0])

    for parity in range(2):
        @pl.when(jnp.logical_and(jnp.logical_and(j > 0, j < nj), j % 2 == parity))
        def _():
            up_proj(acts[parity])
            down_proj(acts[1 - parity])

    @pl.when(j == nj)
    def _():
        down_proj(acts[(nj - 1) % 2])
        y = h_ref[...] + acc_ref[...]
        if final:
            y = _rmsnorm(y, fg_ref[...])
        o_ref[...] = y


def _ffn(h, gain, w_up, conv_w, conv_b, w_down, layer, seq_len, tiles, final_gain=None):
    t, d = h.shape
    ffn = w_down.shape[1]
    tm, tn = min(tiles.ffn_m, seq_len), min(tiles.ffn_n, ffn)
    nj = ffn // tn
    assert seq_len % tm == 0 and ffn % tn == 0
    final = final_gain is not None
    row = lambda m, j: (m, 0)
    const = lambda m, j: (0, 0)
    gate_tile = lambda m, j: (layer, 0, jnp.minimum(j, nj - 1))
    val_tile = lambda m, j: (layer, 0, jnp.minimum(j, nj - 1) + nj)
    in_specs = [
        pl.BlockSpec((tm, d), row),
        pl.BlockSpec((1, d), const),
        pl.BlockSpec((None, d, tn), gate_tile),
        pl.BlockSpec((None, d, tn), val_tile),
        pl.BlockSpec((None, conv_w.shape[1], tn), gate_tile),
        pl.BlockSpec((None, conv_w.shape[1], tn), val_tile),
        pl.BlockSpec((None, 1, tn), gate_tile),
        pl.BlockSpec((None, 1, tn), val_tile),
        pl.BlockSpec((None, tn, d), lambda m, j: (layer, jnp.maximum(j - 1, 0), 0)),
    ]
    conv_b = conv_b.reshape(conv_b.shape[0], 1, -1)
    args = [h, gain.reshape(1, d), w_up, w_up, conv_w, conv_w, conv_b, conv_b, w_down]
    if final:
        in_specs.append(pl.BlockSpec((1, d), const))
        args.append(final_gain.reshape(1, d))
    vmem = (4 * tm * d * 4 + tm * d * 2 + tm * d * 4 + 2 * 3 * d * tn * 2 + 2 * tm * tn * 2
            + 10 * tm * tn * 4 + tm * d * 4)
    return pl.pallas_call(
        functools.partial(_ffn_kernel, tiles_per_seq=seq_len // tm, nj=nj, final=final),
        grid=(t // tm, nj + 1),
        in_specs=in_specs,
        out_specs=pl.BlockSpec((tm, d), row),
        out_shape=jax.ShapeDtypeStruct((t, d), F32),
        scratch_shapes=[
            pltpu.VMEM((tm, d), BF16),
            pltpu.VMEM((tm, d), F32),
            pltpu.VMEM((2 * nj, SUBLANES, tn), F32),
            pltpu.VMEM((tm, tn), BF16),
            pltpu.VMEM((tm, tn), BF16),
            pltpu.VMEM((SUBLANES + tm, tn), F32),
            pltpu.VMEM((SUBLANES + tm, tn), F32),
        ],
        compiler_params=_params(("arbitrary", "arbitrary"), vmem + (8 << 20)),
        name="conv_ffn",
    )(*args)


def _kv_kernel(h_ref, g_ref, wkt_ref, wv_ref, wf_ref, kn_ref, kt_ref, v_ref, f_ref, xn_ref,
               *, head_dim):
    _store_xn(xn_ref, h_ref, g_ref)
    xn = xn_ref[...]
    kt = lax.dot_general(wkt_ref[...], xn, (((1,), (1,)), ((), ())), preferred_element_type=F32)
    for i in range(kt.shape[0] // head_dim):
        rows = slice(i * head_dim, (i + 1) * head_dim)
        x = kt[rows]
        r = lax.rsqrt(jnp.mean(x * x, axis=0, keepdims=True) + EPS)
        kt_ref[rows, :] = ((x * r) * kn_ref[...]).astype(BF16)
    v_ref[...] = _dot(xn, wv_ref[...]).astype(BF16)

    @pl.when(pl.program_id(1) == 0)
    def _():
        f_ref[...] = _dot(xn, wf_ref[...])


def _kv_proj(h, gain, w_kt, w_v, w_f, k_gain, tiles):
    t, d = h.shape
    width = w_v.shape[1]
    head_dim = k_gain.shape[0]
    tm, tn = min(tiles.proj_m, t), min(tiles.proj_n, width)
    vmem = (2 * tm * d * 4 + tm * d * 2 + 2 * 2 * d * tn * 2 + 2 * d * LANES * 2
            + 2 * 2 * tm * tn * 2 + 2 * tm * LANES * 4 + 6 * tm * tn * 4)
    return pl.pallas_call(
        functools.partial(_kv_kernel, head_dim=head_dim),
        grid=(t // tm, width // tn),
        in_specs=[
            pl.BlockSpec((tm, d), lambda m, n: (m, 0)),
            pl.BlockSpec((1, d), lambda m, n: (0, 0)),
            pl.BlockSpec((tn, d), lambda m, n: (n, 0)),
            pl.BlockSpec((d, tn), lambda m, n: (0, n)),
            pl.BlockSpec((d, LANES), lambda m, n: (0, 0)),
            pl.BlockSpec((head_dim, 1), lambda m, n: (0, 0)),
        ],
        out_specs=[
            pl.BlockSpec((tn, tm), lambda m, n: (n, m)),
            pl.BlockSpec((tm, tn), lambda m, n: (m, n)),
            pl.BlockSpec((tm, LANES), lambda m, n: (m, 0)),
        ],
        out_shape=[
            jax.ShapeDtypeStruct((width, t), BF16),
            jax.ShapeDtypeStruct((t, width), BF16),
            jax.ShapeDtypeStruct((t, LANES), F32),
        ],
        scratch_shapes=[pltpu.VMEM((tm, d), BF16)],
        compiler_params=_params(("parallel", "arbitrary"), vmem + (8 << 20)),
        name="kv_proj",
    )(h, gain.reshape(1, d), w_kt, w_v, w_f, k_gain.reshape(head_dim, 1))


def _forget_cumsum_kernel(f_ref, b_ref, c_ref):
    rows, seq = f_ref.shape
    lane = lax.broadcasted_iota(jnp.int32, (rows, LANES), 1)
    carry = jnp.zeros((rows, 1), F32)
    for blk in range(seq // LANES):
        cols = slice(blk * LANES, (blk + 1) * LANES)
        x = _log_sigmoid(f_ref[:, cols] + b_ref[...])
        d = 1
        while d < LANES:
            x = x + jnp.where(lane >= d, pltpu.roll(x, d, axis=1), 0.0)
            d *= 2
        x = x + carry
        c_ref[:, cols] = x
        carry = x[:, LANES - 1:LANES]


def _forget_cumsum(f_rows, b_rows):
    rows, seq = f_rows.shape
    return pl.pallas_call(
        _forget_cumsum_kernel,
        out_shape=jax.ShapeDtypeStruct((rows, seq), F32),
        name="forget_cumsum",
    )(f_rows, b_rows)


def _qg_kernel(h_ref, g_ref, wq_ref, wg_ref, qn_ref, q_ref, sg_ref, xn_ref, *, head_dim):
    _store_xn(xn_ref, h_ref, g_ref)
    xn = xn_ref[...]
    _head_rmsnorm_store(q_ref, _dot(xn, wq_ref[...]), qn_ref[...], head_dim,
                        scale=head_dim ** -0.5 * LOG2E)
    sg_ref[...] = jax.nn.sigmoid(_dot(xn, wg_ref[...])).astype(BF16)


def _qg_proj(h, gain, w_qg, q_gain, layer, tiles):
    t, d = h.shape
    width = w_qg.shape[2] // 2
    head_dim = q_gain.shape[0]
    tm, tn = min(tiles.proj_m, t), min(tiles.proj_n, width)
    nn = width // tn
    vmem = 2 * tm * d * 4 + tm * d * 2 + 2 * 2 * d * tn * 2 + 2 * 2 * tm * tn * 2 + 6 * tm * tn * 4
    return pl.pallas_call(
        functools.partial(_qg_kernel, head_dim=head_dim),
        grid=(t // tm, nn),
        in_specs=[
            pl.BlockSpec((tm, d), lambda m, n: (m, 0)),
            pl.BlockSpec((1, d), lambda m, n: (0, 0)),
            pl.BlockSpec((None, d, tn), lambda m, n: (layer, 0, n)),
            pl.BlockSpec((None, d, tn), lambda m, n: (layer, 0, n + nn)),
            pl.BlockSpec((1, head_dim), lambda m, n: (0, 0)),
        ],
        out_specs=[pl.BlockSpec((tm, tn), lambda m, n: (m, n))] * 2,
        out_shape=[jax.ShapeDtypeStruct((t, width), BF16)] * 2,
        scratch_shapes=[pltpu.VMEM((tm, d), BF16)],
        compiler_params=_params(("parallel", "arbitrary"), vmem + (8 << 20)),
        name="qg_proj",
    )(h, gain.reshape(1, d), w_qg, w_qg, q_gain.reshape(1, head_dim))


C_TERMS = 3


def _split_bf16_terms(x):
    terms = []
    for _ in range(C_TERMS):
        t = x.astype(BF16).astype(F32)
        terms.append(t)
        x = x - t
    return terms


def _attn_kernel(q_ref, kt_ref, v_ref, crow_ref, ccol_ref, sg_ref, o_ref,
                 qa_ref, kta_ref, va_ref, m_ref, acc_ref, *, heads, head_dim):
    blk = q_ref.shape[0]
    nk = kta_ref.shape[1]
    hd = head_dim
    qi = pl.program_id(2)

    @pl.when(qi == 0)
    def _():
        sub = lax.broadcasted_iota(jnp.int32, (2 * SUBLANES, blk), 0)
        for h in range(heads):
            for kb in range(nk):
                cols = slice(kb * blk, (kb + 1) * blk)
                c_s = _split_bf16_terms(crow_ref[h, :, cols] * LOG2E)
                tail = jnp.where(sub < C_TERMS, 1.0, 0.0)
                for j, term in enumerate(c_s):
                    tail = jnp.where(sub == C_TERMS + j, -term, tail)
                kta_ref[h, kb, :hd, :] = kt_ref[h * hd:(h + 1) * hd, cols]
                kta_ref[h, kb, hd:hd + 2 * SUBLANES, :] = tail.astype(BF16)
                kta_ref[h, kb, hd + 2 * SUBLANES:, :] = jnp.zeros((hd - 2 * SUBLANES, blk), BF16)
            va_ref[h, :, :hd] = v_ref[:, h * hd:(h + 1) * hd]
            va_ref[h, :, hd:] = jnp.ones((v_ref.shape[0], hd), BF16)

    lane = lax.broadcasted_iota(jnp.int32, (blk, hd), 1)
    for h in range(heads):
        c_t = _split_bf16_terms(ccol_ref[h] * LOG2E)
        tail = jnp.where(jnp.logical_and(lane >= C_TERMS, lane < 2 * C_TERMS), 1.0, 0.0)
        for j, term in enumerate(c_t):
            tail = jnp.where(lane == j, term, tail)
        qa_ref[h, :, :hd] = q_ref[:, h * hd:(h + 1) * hd]
        qa_ref[h, :, hd:] = tail.astype(BF16)

    def blocks(k0, count, diagonal):
        start = pl.multiple_of(k0 * blk, blk)
        for h in range(heads):
            chunks = []
            for b in range(count):
                s = _dot(qa_ref[h], kta_ref[h, k0 + b])
                if diagonal:
                    row = lax.broadcasted_iota(jnp.int32, (blk, blk), 0)
                    col = lax.broadcasted_iota(jnp.int32, (blk, blk), 1)
                    s = jnp.where(row >= col, s, -jnp.inf)
                chunks += [s[:, c * LANES:(c + 1) * LANES] for c in range(blk // LANES)]
            m_new = jnp.max(functools.reduce(jnp.maximum, chunks), axis=1, keepdims=True)
            if diagonal:
                m_new = jnp.broadcast_to(m_new, (blk, LANES))
            else:
                m_old = m_ref[h]
                m_new = jnp.maximum(m_old, m_new)
            p = jnp.concatenate([jnp.exp2(c - m_new).astype(BF16) for c in chunks], axis=1)
            pv = _dot(p, va_ref[h, pl.ds(start, count * blk), :])
            if diagonal:
                acc_ref[h] = pv
            else:
                alpha = jnp.exp2(m_old - m_new)
                acc_ref[h] = jnp.concatenate([alpha] * (2 * hd // LANES), axis=1) * acc_ref[h] + pv
            m_ref[h] = m_new

    blocks(qi, 1, True)
    odd = qi % 2

    @pl.when(odd == 1)
    def _():
        blocks(0, 1, False)

    def pair(i, carry):
        blocks(2 * i + odd, 2, False)
        return carry

    lax.fori_loop(0, qi // 2, pair, 0)
    for h in range(heads):
        acc = acc_ref[h]
        o = acc[:, :hd] / acc[:, hd:]
        cols = slice(h * hd, (h + 1) * hd)
        o_ref[:, cols] = (o * sg_ref[:, cols].astype(F32)).astype(o_ref.dtype)


def _attention(q, kt, v, c, sg, batch, heads, tiles):
    t, width = q.shape
    seq = t // batch
    hd = width // heads
    assert hd == LANES
    blk = min(tiles.attn, seq)
    nq = seq // blk
    hp = min(tiles.attn_heads, heads)
    c_row = c.reshape(batch * heads, 1, seq)
    c_col = c.reshape(batch * heads, seq, 1)
    qmap = lambda b, g, i: (b * nq + i, g)
    vmem = (2 * 2 * seq * hp * hd * 2 + 2 * seq * hp * 2 * hd * 2 + 2 * 3 * blk * hp * hd * 2
            + hp * blk * (2 * hd * 2 + LANES * 4 + 2 * hd * 4) + 2 * hp * seq * 4 * SUBLANES
            + 2 * hp * blk * LANES * 4 + hp * 3 * blk * blk * 4)
    return pl.pallas_call(
        functools.partial(_attn_kernel, heads=hp, head_dim=hd),
        grid=(batch, heads // hp, nq),
        in_specs=[
            pl.BlockSpec((blk, hp * hd), qmap),
            pl.BlockSpec((hp * hd, seq), lambda b, g, i: (g, b)),
            pl.BlockSpec((seq, hp * hd), lambda b, g, i: (b, g)),
            pl.BlockSpec((hp, 1, seq), lambda b, g, i: (b * (heads // hp) + g, 0, 0)),
            pl.BlockSpec((hp, blk, 1), lambda b, g, i: (b * (heads // hp) + g, i, 0)),
            pl.BlockSpec((blk, hp * hd), qmap),
        ],
        out_specs=pl.BlockSpec((blk, hp * hd), qmap),
        out_shape=jax.ShapeDtypeStruct((t, width), BF16),
        scratch_shapes=[
            pltpu.VMEM((hp, blk, 2 * hd), BF16),
            pltpu.VMEM((hp, nq, 2 * hd, blk), BF16),
            pltpu.VMEM((hp, seq, 2 * hd), BF16),
            pltpu.VMEM((hp, blk, LANES), F32),
            pltpu.VMEM((hp, blk, 2 * hd), F32),
        ],
        compiler_params=_params(("parallel", "parallel", "arbitrary"), vmem + (8 << 20)),
        name="fox_attention",
    )(q, kt, v, c_row, c_col, sg)


def _out_proj_kernel(h_ref, a_ref, w_ref, o_ref):
    o_ref[...] = h_ref[...] + _dot(a_ref[...], w_ref[...])


def _out_proj(h, a, w, layer, tiles):
    t, d = h.shape
    kdim = a.shape[1]
    tm, tn = min(tiles.proj_m, t), min(tiles.proj_n, d)
    vmem = 4 * tm * tn * 4 + 2 * tm * kdim * 2 + 2 * kdim * tn * 2 + 2 * tm * tn * 4
    return pl.pallas_call(
        _out_proj_kernel,
        grid=(t // tm, d // tn),
        in_specs=[
            pl.BlockSpec((tm, tn), lambda m, n: (m, n)),
            pl.BlockSpec((tm, kdim), lambda m, n: (m, 0)),
            pl.BlockSpec((None, kdim, tn), lambda m, n: (layer, 0, n)),
        ],
        out_specs=pl.BlockSpec((tm, tn), lambda m, n: (m, n)),
        out_shape=jax.ShapeDtypeStruct((t, d), F32),
        compiler_params=_params(("parallel", "arbitrary"), vmem + (8 << 20)),
        name="b_out_proj",
    )(h, a, w)


def _forward(x, a_norm, a_w_in, a_v_norm, a_w_s, a_b_s, a_w_out, kv_norm, w_kvf, b_f, k_norm,
             b_norm, b_w_qg, q_norm, b_w_out, f_norm, f_w_up, f_conv_w, f_conv_b, f_w_down,
             final_norm, tiles=Tiles()):
    batch, seq, d = x.shape
    depth = f_norm.shape[0]
    n_a = a_norm.shape[0]
    heads = b_f.shape[0]
    att = heads * k_norm.shape[0]
    h = x.reshape(batch * seq, d)
    bf = lambda w: w.astype(BF16)
    a_w_in, a_w_s, a_w_out = bf(a_w_in), bf(a_w_s), bf(a_w_out)
    b_w_qg, b_w_out = bf(b_w_qg), bf(b_w_out)
    f_w_up, f_w_down = bf(f_w_up), bf(f_w_down)
    k = v = c = None
    for l in range(depth):
        if l < n_a:
            u, vv = _a_in(h, a_norm[l], a_w_in, l, tiles)
            h = _a_gate(h, u, vv, a_v_norm[l], a_w_s, a_b_s[l], a_w_out, l, tiles)
        else:
            j = l - n_a
            q, sg = _qg_proj(h, b_norm[j], b_w_qg, q_norm[j], j, tiles)
            og = _attention(q, k, v, c, sg, batch, heads, tiles)
            h = _out_proj(h, og, b_w_out, j, tiles)
        h = _ffn(h, f_norm[l], f_w_up, f_conv_w, f_conv_b, f_w_down, l, seq, tiles,
                 final_gain=final_norm if l == depth - 1 else None)
        if l == n_a - 1:
            w_f = jnp.pad(w_kvf[:, 2 * att:], ((0, 0), (0, LANES - heads)))
            k, v, f = _kv_proj(h, kv_norm, bf(w_kvf[:, :att]).T, bf(w_kvf[:, att:2 * att]),
                               bf(w_f), k_norm, tiles)
            f_rows = f[:, :heads].reshape(batch, seq, heads).transpose(0, 2, 1)
            c = _forget_cumsum(f_rows.reshape(batch * heads, seq),
                               jnp.tile(b_f, batch).reshape(batch * heads, 1))
    return h.reshape(batch, seq, d)


def kernel(x, a_norm, a_w_in, a_v_norm, a_w_s, a_b_s, a_w_out, kv_norm, w_kvf, b_f, k_norm,
           b_norm, b_w_qg, q_norm, b_w_out, f_norm, f_w_up, f_conv_w, f_conv_b, f_w_down,
           final_norm):
    return _forward(x, a_norm, a_w_in, a_v_norm, a_w_s, a_b_s, a_w_out, kv_norm, w_kvf, b_f,
                    k_norm, b_norm, b_w_qg, q_norm, b_w_out, f_norm, f_w_up, f_conv_w, f_conv_b,
                    f_w_down, final_norm)
```

```python
import functools
from typing import NamedTuple

import jax
import jax.numpy as jnp
from jax import lax
from jax.experimental import pallas as pl
from jax.experimental.pallas import tpu as pltpu

F32 = jnp.float32
BF16 = jnp.bfloat16
EPS = 1e-6
LOG2E = 1.4426950408889634
SUBLANES = 8
LANES = 128
MXU_COLS = 256
V7X_VMEM_BYTES = 64 * 1024 * 1024


class Tiles(NamedTuple):
    proj_m: int = 1024
    proj_n: int = 512
    gate_m: int = 512
    ffn_m: int = 1024
    ffn_n: int = 512
    attn: int = 512
    attn_heads: int = 4


def _params(semantics, vmem_bytes):
    return pltpu.CompilerParams(dimension_semantics=semantics,
                                vmem_limit_bytes=min(vmem_bytes, V7X_VMEM_BYTES - (4 << 20)))


def _rmsnorm(x, gain):
    r = lax.rsqrt(jnp.mean(x * x, axis=-1, keepdims=True) + EPS)
    return (x * r) * gain


def _head_rmsnorm_store(o_ref, x, gain, head_dim, scale=None):
    for i in range(x.shape[1] // head_dim):
        xh = x[:, i * head_dim:(i + 1) * head_dim]
        y = _rmsnorm(xh, gain)
        if scale is not None:
            y = y * scale
        o_ref[:, i * head_dim:(i + 1) * head_dim] = y.astype(o_ref.dtype)


def _gelu(x):
    return 0.5 * x * (1.0 + lax.erf(x * (0.5 ** 0.5)))


def _log_sigmoid(x):
    return jnp.minimum(x, 0.0) - jnp.log1p(jnp.exp(-jnp.abs(x)))


def _dot(a, b):
    return jnp.dot(a, b, preferred_element_type=F32)


def _store_xn(xn_ref, h_ref, g_ref):
    @pl.when(pl.program_id(1) == 0)
    def _():
        xn_ref[...] = _rmsnorm(h_ref[...], g_ref[...]).astype(BF16)


def _a_in_kernel(h_ref, g_ref, wu_ref, wv_ref, u_ref, v_ref, xn_ref):
    _store_xn(xn_ref, h_ref, g_ref)
    xn = xn_ref[...]
    u_ref[...] = _gelu(_dot(xn, wu_ref[...])).astype(BF16)
    v_ref[...] = _gelu(_dot(xn, wv_ref[...])).astype(BF16)


def _a_in(h, gain, w_in, layer, tiles):
    t, d = h.shape
    width = w_in.shape[2] // 2
    tm, tn = min(tiles.proj_m, t), min(tiles.proj_n, width)
    nn = width // tn
    vmem = 2 * tm * d * 4 + tm * d * 2 + 2 * 2 * d * tn * 2 + 2 * 2 * tm * tn * 2 + 6 * tm * tn * 4
    return pl.pallas_call(
        _a_in_kernel,
        grid=(t // tm, nn),
        in_specs=[
            pl.BlockSpec((tm, d), lambda m, n: (m, 0)),
            pl.BlockSpec((1, d), lambda m, n: (0, 0)),
            pl.BlockSpec((None, d, tn), lambda m, n: (layer, 0, n)),
            pl.BlockSpec((None, d, tn), lambda m, n: (layer, 0, n + nn)),
        ],
        out_specs=[pl.BlockSpec((tm, tn), lambda m, n: (m, n))] * 2,
        out_shape=[jax.ShapeDtypeStruct((t, width), BF16)] * 2,
        scratch_shapes=[pltpu.VMEM((tm, d), BF16)],
        compiler_params=_params(("parallel", "arbitrary"), vmem + (8 << 20)),
        name="a_in_proj",
    )(h, gain.reshape(1, d), w_in, w_in)


def _a_gate_kernel(h_ref, u_ref, v_ref, gn_ref, ws_ref, bt_ref, wo_ref, o_ref, gated_ref,
                   *, chunk, groups):
    tm, width = v_ref.shape
    gdim = width // groups
    row = lax.broadcasted_iota(jnp.int32, (chunk, chunk), 0)
    col = lax.broadcasted_iota(jnp.int32, (chunk, chunk), 1)
    causal = row >= col
    parts = 2 if (tm // chunk) % 2 == 0 else 1
    nchunks = tm // chunk // parts
    for part in range(parts):
        base = part * nchunks * chunk
        part_rows = slice(base, base + nchunks * chunk)
        vs = _rmsnorm(v_ref[part_rows, :].astype(F32), gn_ref[...]).astype(BF16)
        for g in range(groups):
            w = jnp.where(causal, ws_ref[g], jnp.zeros((), BF16))
            bias = bt_ref[:, g:g + 1]
            cols = slice(g * gdim, (g + 1) * gdim)
            rhs = jnp.concatenate([vs[c * chunk:(c + 1) * chunk, cols] for c in range(nchunks)],
                                  axis=1)
            mixed = _dot(w, rhs) + bias
            for c in range(nchunks):
                rows = slice(base + c * chunk, base + (c + 1) * chunk)
                gated_ref[rows, cols] = (u_ref[rows, cols].astype(F32)
                                         * mixed[:, c * gdim:(c + 1) * gdim]).astype(BF16)
        o_ref[part_rows, :] = h_ref[part_rows, :] + _dot(gated_ref[part_rows, :], wo_ref[...])


def _a_gate(h, u, v, v_gain, w_s, b_s, w_out, layer, tiles):
    t, d = h.shape
    width = u.shape[1]
    _, groups, chunk, _ = w_s.shape
    tm = min(tiles.gate_m, t)
    vmem = (4 * tm * d * 4 + 4 * tm * width * 2 + 2 * width * d * 2 + tm * width * 2
            + 3 * tm * width * 4 + tm * d * 4)
    return pl.pallas_call(
        functools.partial(_a_gate_kernel, chunk=chunk, groups=groups),
        grid=(t // tm,),
        in_specs=[
            pl.BlockSpec((tm, d), lambda m: (m, 0)),
            pl.BlockSpec((tm, width), lambda m: (m, 0)),
            pl.BlockSpec((tm, width), lambda m: (m, 0)),
            pl.BlockSpec((1, width), lambda m: (0, 0)),
            pl.BlockSpec((None, groups, chunk, chunk), lambda m: (layer, 0, 0, 0)),
            pl.BlockSpec((chunk, groups), lambda m: (0, 0)),
            pl.BlockSpec((None, width, d), lambda m: (layer, 0, 0), pipeline_mode=pl.Buffered(1)),
        ],
        out_specs=pl.BlockSpec((tm, d), lambda m: (m, 0)),
        out_shape=jax.ShapeDtypeStruct((t, d), F32),
        scratch_shapes=[pltpu.VMEM((tm, width), BF16)],
        compiler_params=_params(("parallel",), vmem + (8 << 20)),
        name="a_gate_out_proj",
    )(h, u, v, v_gain.reshape(1, width), w_s, b_s.T, w_out)


def _ffn_kernel(*refs, tiles_per_seq, nj, final):
    if final:
        (h_ref, g_ref, wg_ref, wv_ref, cwg_ref, cwv_ref, cbg_ref, cbv_ref, wd_ref, fg_ref,
         o_ref, xn_ref, acc_ref, carry_ref, act0_ref, act1_ref, upg_ref, upv_ref) = refs
    else:
        (h_ref, g_ref, wg_ref, wv_ref, cwg_ref, cwv_ref, cbg_ref, cbv_ref, wd_ref,
         o_ref, xn_ref, acc_ref, carry_ref, act0_ref, act1_ref, upg_ref, upv_ref) = refs
    acts = (act0_ref, act1_ref)
    m, j = pl.program_id(0), pl.program_id(1)
    tm, d = h_ref.shape
    tn = wg_ref.shape[1]
    seq_start = (m % tiles_per_seq) == 0

    @pl.when(j == 0)
    def _():
        xn_ref[...] = _rmsnorm(h_ref[...], g_ref[...]).astype(BF16)
        acc_ref[...] = jnp.zeros(acc_ref.shape, F32)

    @pl.when(jnp.logical_and(seq_start, j < nj))
    def _():
        carry_ref[2 * j] = jnp.zeros(carry_ref.shape[1:], F32)
        carry_ref[2 * j + 1] = jnp.zeros(carry_ref.shape[1:], F32)

    def conv(w_ref, cw_ref, cb_ref, slot, up_ref):
        up_ref[:SUBLANES] = carry_ref[slot]
        up = _dot(xn_ref[...], w_ref[...])
        up_ref[SUBLANES:] = up
        carry_ref[slot] = up[tm - SUBLANES:, :]
        w0, w1, w2, b = cw_ref[0:1], cw_ref[1:2], cw_ref[2:3], cb_ref[...]
        return (up * w2 + up_ref[SUBLANES - 1:SUBLANES - 1 + tm] * w1
                + up_ref[SUBLANES - 2:SUBLANES - 2 + tm] * w0 + b)

    def up_proj(act_ref):
        gate = conv(wg_ref, cwg_ref, cbg_ref, 2 * j, upg_ref)
        val = conv(wv_ref, cwv_ref, cbv_ref, 2 * j + 1, upv_ref)
        act_ref[...] = (jax.nn.silu(gate) * val).astype(BF16)

    def down_proj(act_ref):
        acc_ref[...] += _dot(act_ref[...], wd_ref[...])

    @pl.when(j == 0)
    def _():
        up_proj(acts[0])

    for parity in range(2):
        @pl.when(jnp.logical_and(jnp.logical_and(j > 0, j < nj), j % 2 == parity))
        def _():
            up_proj(acts[parity])
            down_proj(acts[1 - parity])

    @pl.when(j == nj)
    def _():
        down_proj(acts[(nj - 1) % 2])
        y = h_ref[...] + acc_ref[...]
        if final:
            y = _rmsnorm(y, fg_ref[...])
        o_ref[...] = y


def _ffn(h, gain, w_up, conv_w, conv_b, w_down, layer, seq_len, tiles, final_gain=None):
    t, d = h.shape
    ffn = w_down.shape[1]
    tm, tn = min(tiles.ffn_m, seq_len), min(tiles.ffn_n, ffn)
    nj = ffn // tn
    assert seq_len % tm == 0 and ffn % tn == 0
    final = final_gain is not None
    row = lambda m, j: (m, 0)
    const = lambda m, j: (0, 0)
    gate_tile = lambda m, j: (layer, 0, jnp.minimum(j, nj - 1))
    val_tile = lambda m, j: (layer, 0, jnp.minimum(j, nj - 1) + nj)
    in_specs = [
        pl.BlockSpec((tm, d), row, pipeline_mode=pl.Buffered(1)),
        pl.BlockSpec((1, d), const),
        pl.BlockSpec((None, d, tn), gate_tile),
        pl.BlockSpec((None, d, tn), val_tile),
        pl.BlockSpec((None, conv_w.shape[1], tn), gate_tile),
        pl.BlockSpec((None, conv_w.shape[1], tn), val_tile),
        pl.BlockSpec((None, 1, tn), gate_tile),
        pl.BlockSpec((None, 1, tn), val_tile),
        pl.BlockSpec((None, tn, d), lambda m, j: (layer, jnp.maximum(j - 1, 0), 0)),
    ]
    conv_b = conv_b.reshape(conv_b.shape[0], 1, -1)
    args = [h, gain.reshape(1, d), w_up, w_up, conv_w, conv_w, conv_b, conv_b, w_down]
    if final:
        in_specs.append(pl.BlockSpec((1, d), const))
        args.append(final_gain.reshape(1, d))
    vmem = (2 * tm * d * 4 + tm * d * 2 + tm * d * 4 + 2 * 3 * d * tn * 2 + 2 * tm * tn * 2
            + 8 * tm * tn * 4)
    return pl.pallas_call(
        functools.partial(_ffn_kernel, tiles_per_seq=seq_len // tm, nj=nj, final=final),
        grid=(t // tm, nj + 1),
        in_specs=in_specs,
        out_specs=pl.BlockSpec((tm, d), row, pipeline_mode=pl.Buffered(1)),
        out_shape=jax.ShapeDtypeStruct((t, d), F32),
        scratch_shapes=[
            pltpu.VMEM((tm, d), BF16),
            pltpu.VMEM((tm, d), F32),
            pltpu.VMEM((2 * nj, SUBLANES, tn), F32),
            pltpu.VMEM((tm, tn), BF16),
            pltpu.VMEM((tm, tn), BF16),
            pltpu.VMEM((SUBLANES + tm, tn), F32),
            pltpu.VMEM((SUBLANES + tm, tn), F32),
        ],
        compiler_params=_params(("arbitrary", "arbitrary"), vmem + (8 << 20)),
        name="conv_ffn",
    )(*args)


def _kv_kernel(h_ref, g_ref, wkt_ref, wv_ref, wf_ref, kn_ref, kt_ref, v_ref, f_ref, xn_ref,
               *, head_dim):
    _store_xn(xn_ref, h_ref, g_ref)
    xn = xn_ref[...]
    kt = lax.dot_general(wkt_ref[...], xn, (((1,), (1,)), ((), ())), preferred_element_type=F32)
    for i in range(kt.shape[0] // head_dim):
        rows = slice(i * head_dim, (i + 1) * head_dim)
        x = kt[rows]
        r = lax.rsqrt(jnp.mean(x * x, axis=0, keepdims=True) + EPS)
        kt_ref[rows, :] = ((x * r) * kn_ref[...]).astype(BF16)
    v_ref[...] = _dot(xn, wv_ref[...]).astype(BF16)

    @pl.when(pl.program_id(1) == 0)
    def _():
        f_ref[...] = _dot(xn, wf_ref[...])


def _kv_proj(h, gain, w_kt, w_v, w_f, k_gain, tiles):
    t, d = h.shape
    width = w_v.shape[1]
    head_dim = k_gain.shape[0]
    tm, tn = min(tiles.proj_m, t), min(tiles.proj_n, width)
    vmem = (2 * tm * d * 4 + tm * d * 2 + 2 * 2 * d * tn * 2 + 2 * d * LANES * 2
            + 2 * 2 * tm * tn * 2 + 2 * tm * LANES * 4 + 6 * tm * tn * 4)
    return pl.pallas_call(
        functools.partial(_kv_kernel, head_dim=head_dim),
        grid=(t // tm, width // tn),
        in_specs=[
            pl.BlockSpec((tm, d), lambda m, n: (m, 0)),
            pl.BlockSpec((1, d), lambda m, n: (0, 0)),
            pl.BlockSpec((tn, d), lambda m, n: (n, 0)),
            pl.BlockSpec((d, tn), lambda m, n: (0, n)),
            pl.BlockSpec((d, LANES), lambda m, n: (0, 0)),
            pl.BlockSpec((head_dim, 1), lambda m, n: (0, 0)),
        ],
        out_specs=[
            pl.BlockSpec((tn, tm), lambda m, n: (n, m)),
            pl.BlockSpec((tm, tn), lambda m, n: (m, n)),
            pl.BlockSpec((tm, LANES), lambda m, n: (m, 0)),
        ],
        out_shape=[
            jax.ShapeDtypeStruct((width, t), BF16),
            jax.ShapeDtypeStruct((t, width), BF16),
            jax.ShapeDtypeStruct((t, LANES), F32),
        ],
        scratch_shapes=[pltpu.VMEM((tm, d), BF16)],
        compiler_params=_params(("parallel", "arbitrary"), vmem + (8 << 20)),
        name="kv_proj",
    )(h, gain.reshape(1, d), w_kt, w_v, w_f, k_gain.reshape(head_dim, 1))


def _forget_cumsum_kernel(f_ref, b_ref, c_ref):
    rows, seq = f_ref.shape
    lane = lax.broadcasted_iota(jnp.int32, (rows, LANES), 1)
    carry = jnp.zeros((rows, 1), F32)
    for blk in range(seq // LANES):
        cols = slice(blk * LANES, (blk + 1) * LANES)
        x = _log_sigmoid(f_ref[:, cols] + b_ref[...])
        d = 1
        while d < LANES:
            x = x + jnp.where(lane >= d, pltpu.roll(x, d, axis=1), 0.0)
            d *= 2
        x = x + carry
        c_ref[:, cols] = x
        carry = x[:, LANES - 1:LANES]


def _forget_cumsum(f_rows, b_rows):
    rows, seq = f_rows.shape
    return pl.pallas_call(
        _forget_cumsum_kernel,
        out_shape=jax.ShapeDtypeStruct((rows, seq), F32),
        name="forget_cumsum",
    )(f_rows, b_rows)


def _qg_kernel(h_ref, g_ref, wq_ref, wg_ref, qn_ref, q_ref, sg_ref, xn_ref, *, head_dim):
    _store_xn(xn_ref, h_ref, g_ref)
    xn = xn_ref[...]
    _head_rmsnorm_store(q_ref, _dot(xn, wq_ref[...]), qn_ref[...], head_dim,
                        scale=head_dim ** -0.5 * LOG2E)
    sg_ref[...] = jax.nn.sigmoid(_dot(xn, wg_ref[...])).astype(BF16)


def _qg_proj(h, gain, w_qg, q_gain, layer, tiles):
    t, d = h.shape
    width = w_qg.shape[2] // 2
    head_dim = q_gain.shape[0]
    tm, tn = min(tiles.proj_m, t), min(tiles.proj_n, width)
    nn = width // tn
    vmem = 2 * tm * d * 4 + tm * d * 2 + 2 * 2 * d * tn * 2 + 2 * 2 * tm * tn * 2 + 6 * tm * tn * 4
    return pl.pallas_call(
        functools.partial(_qg_kernel, head_dim=head_dim),
        grid=(t // tm, nn),
        in_specs=[
            pl.BlockSpec((tm, d), lambda m, n: (m, 0)),
            pl.BlockSpec((1, d), lambda m, n: (0, 0)),
            pl.BlockSpec((None, d, tn), lambda m, n: (layer, 0, n)),
            pl.BlockSpec((None, d, tn), lambda m, n: (layer, 0, n + nn)),
            pl.BlockSpec((1, head_dim), lambda m, n: (0, 0)),
        ],
        out_specs=[pl.BlockSpec((tm, tn), lambda m, n: (m, n))] * 2,
        out_shape=[jax.ShapeDtypeStruct((t, width), BF16)] * 2,
        scratch_shapes=[pltpu.VMEM((tm, d), BF16)],
        compiler_params=_params(("parallel", "arbitrary"), vmem + (8 << 20)),
        name="qg_proj",
    )(h, gain.reshape(1, d), w_qg, w_qg, q_gain.reshape(1, head_dim))


C_TERMS = 3


def _split_bf16_terms(x):
    terms = []
    for _ in range(C_TERMS):
        t = x.astype(BF16).astype(F32)
        terms.append(t)
        x = x - t
    return terms


def _attn_kernel(q_ref, kt_ref, v_ref, crow_ref, ccol_ref, sg_ref, o_ref,
                 qa_ref, kta_ref, va_ref, m_ref, acc_ref, *, heads, head_dim):
    blk = q_ref.shape[0]
    nk = kta_ref.shape[1]
    hd = head_dim
    qi = pl.program_id(2)

    @pl.when(qi == 0)
    def _():
        sub = lax.broadcasted_iota(jnp.int32, (2 * SUBLANES, blk), 0)
        for h in range(heads):
            for kb in range(nk):
                cols = slice(kb * blk, (kb + 1) * blk)
                c_s = _split_bf16_terms(crow_ref[h, :, cols] * LOG2E)
                tail = jnp.where(sub < C_TERMS, 1.0, 0.0)
                for j, term in enumerate(c_s):
                    tail = jnp.where(sub == C_TERMS + j, -term, tail)
                kta_ref[h, kb, :hd, :] = kt_ref[h * hd:(h + 1) * hd, cols]
                kta_ref[h, kb, hd:hd + 2 * SUBLANES, :] = tail.astype(BF16)
                kta_ref[h, kb, hd + 2 * SUBLANES:, :] = jnp.zeros((hd - 2 * SUBLANES, blk), BF16)
            va_ref[h, :, :hd] = v_ref[:, h * hd:(h + 1) * hd]
            va_ref[h, :, hd:] = jnp.ones((v_ref.shape[0], hd), BF16)

    lane = lax.broadcasted_iota(jnp.int32, (blk, hd), 1)
    for h in range(heads):
        c_t = _split_bf16_terms(ccol_ref[h] * LOG2E)
        tail = jnp.where(jnp.logical_and(lane >= C_TERMS, lane < 2 * C_TERMS), 1.0, 0.0)
        for j, term in enumerate(c_t):
            tail = jnp.where(lane == j, term, tail)
        qa_ref[h, :, :hd] = q_ref[:, h * hd:(h + 1) * hd]
        qa_ref[h, :, hd:] = tail.astype(BF16)

    def blocks(k0, count, diagonal):
        start = pl.multiple_of(k0 * blk, blk)
        for h in range(heads):
            chunks = []
            for b in range(count):
                s = _dot(qa_ref[h], kta_ref[h, k0 + b])
                if diagonal:
                    row = lax.broadcasted_iota(jnp.int32, (blk, blk), 0)
                    col = lax.broadcasted_iota(jnp.int32, (blk, blk), 1)
                    s = jnp.where(row >= col, s, -jnp.inf)
                chunks += [s[:, c * LANES:(c + 1) * LANES] for c in range(blk // LANES)]
            m_new = jnp.max(functools.reduce(jnp.maximum, chunks), axis=1, keepdims=True)
            if diagonal:
                m_new = jnp.broadcast_to(m_new, (blk, LANES))
            else:
                m_old = m_ref[h]
                m_new = jnp.maximum(m_old, m_new)
            p = jnp.concatenate([jnp.exp2(c - m_new).astype(BF16) for c in chunks], axis=1)
            pv = _dot(p, va_ref[h, pl.ds(start, count * blk), :])
            if diagonal:
                acc_ref[h] = pv
            else:
                alpha = jnp.exp2(m_old - m_new)
                acc_ref[h] = jnp.concatenate([alpha] * (2 * hd // LANES), axis=1) * acc_ref[h] + pv
            m_ref[h] = m_new

    blocks(qi, 1, True)
    odd = qi % 2

    @pl.when(odd == 1)
    def _():
        blocks(0, 1, False)

    def pair(i, carry):
        blocks(2 * i + odd, 2, False)
        return carry

    lax.fori_loop(0, qi // 2, pair, 0)
    for h in range(heads):
        acc = acc_ref[h]
        o = acc[:, :hd] / acc[:, hd:]
        cols = slice(h * hd, (h + 1) * hd)
        o_ref[:, cols] = (o * sg_ref[:, cols].astype(F32)).astype(o_ref.dtype)


def _attention(q, kt, v, c, sg, batch, heads, tiles):
    t, width = q.shape
    seq = t // batch
    hd = width // heads
    assert hd == LANES
    blk = min(tiles.attn, seq)
    nq = seq // blk
    hp = min(tiles.attn_heads, heads)
    c_row = c.reshape(batch * heads, 1, seq)
    c_col = c.reshape(batch * heads, seq, 1)
    qmap = lambda b, g, i: (b * nq + i, g)
    vmem = (2 * 2 * seq * hp * hd * 2 + 2 * seq * hp * 2 * hd * 2 + 2 * 3 * blk * hp * hd * 2
            + hp * blk * (2 * hd * 2 + LANES * 4 + 2 * hd * 4) + 2 * hp * seq * 4 * SUBLANES
            + 2 * hp * blk * LANES * 4 + hp * 3 * blk * blk * 4)
    return pl.pallas_call(
        functools.partial(_attn_kernel, heads=hp, head_dim=hd),
        grid=(batch, heads // hp, nq),
        in_specs=[
            pl.BlockSpec((blk, hp * hd), qmap),
            pl.BlockSpec((hp * hd, seq), lambda b, g, i: (g, b)),
            pl.BlockSpec((seq, hp * hd), lambda b, g, i: (b, g)),
            pl.BlockSpec((hp, 1, seq), lambda b, g, i: (b * (heads // hp) + g, 0, 0)),
            pl.BlockSpec((hp, blk, 1), lambda b, g, i: (b * (heads // hp) + g, i, 0)),
            pl.BlockSpec((blk, hp * hd), qmap),
        ],
        out_specs=pl.BlockSpec((blk, hp * hd), qmap),
        out_shape=jax.ShapeDtypeStruct((t, width), BF16),
        scratch_shapes=[
            pltpu.VMEM((hp, blk, 2 * hd), BF16),
            pltpu.VMEM((hp, nq, 2 * hd, blk), BF16),
            pltpu.VMEM((hp, seq, 2 * hd), BF16),
            pltpu.VMEM((hp, blk, LANES), F32),
            pltpu.VMEM((hp, blk, 2 * hd), F32),
        ],
        compiler_params=_params(("parallel", "parallel", "arbitrary"), vmem + (8 << 20)),
        name="fox_attention",
    )(q, kt, v, c_row, c_col, sg)


def _out_proj_kernel(h_ref, a_ref, w_ref, o_ref):
    o_ref[...] = h_ref[...] + _dot(a_ref[...], w_ref[...])


def _out_proj(h, a, w, layer, tiles):
    t, d = h.shape
    kdim = a.shape[1]
    tm = min(tiles.gate_m, t)
    vmem = 4 * tm * d * 4 + 2 * tm * kdim * 2 + kdim * d * 2 + 2 * tm * d * 4
    return pl.pallas_call(
        _out_proj_kernel,
        grid=(t // tm,),
        in_specs=[
            pl.BlockSpec((tm, d), lambda m: (m, 0)),
            pl.BlockSpec((tm, kdim), lambda m: (m, 0)),
            pl.BlockSpec((None, kdim, d), lambda m: (layer, 0, 0), pipeline_mode=pl.Buffered(1)),
        ],
        out_specs=pl.BlockSpec((tm, d), lambda m: (m, 0)),
        out_shape=jax.ShapeDtypeStruct((t, d), F32),
        compiler_params=_params(("parallel",), vmem + (8 << 20)),
        name="b_out_proj",
    )(h, a, w)


def _forward(x, a_norm, a_w_in, a_v_norm, a_w_s, a_b_s, a_w_out, kv_norm, w_kvf, b_f, k_norm,
             b_norm, b_w_qg, q_norm, b_w_out, f_norm, f_w_up, f_conv_w, f_conv_b, f_w_down,
             final_norm, tiles=Tiles()):
    batch, seq, d = x.shape
    depth = f_norm.shape[0]
    n_a = a_norm.shape[0]
    heads = b_f.shape[0]
    att = heads * k_norm.shape[0]
    h = x.reshape(batch * seq, d)
    bf = lambda w: w.astype(BF16)
    a_w_in, a_w_s, a_w_out = bf(a_w_in), bf(a_w_s), bf(a_w_out)
    b_w_qg, b_w_out = bf(b_w_qg), bf(b_w_out)
    f_w_up, f_w_down = bf(f_w_up), bf(f_w_down)
    k = v = c = None
    for l in range(depth):
        if l < n_a:
            u, vv = _a_in(h, a_norm[l], a_w_in, l, tiles)
            h = _a_gate(h, u, vv, a_v_norm[l], a_w_s, a_b_s[l], a_w_out, l, tiles)
        else:
            j = l - n_a
            q, sg = _qg_proj(h, b_norm[j], b_w_qg, q_norm[j], j, tiles)
            og = _attention(q, k, v, c, sg, batch, heads, tiles)
            h = _out_proj(h, og, b_w_out, j, tiles)
        h = _ffn(h, f_norm[l], f_w_up, f_conv_w, f_conv_b, f_w_down, l, seq, tiles,
                 final_gain=final_norm if l == depth - 1 else None)
        if l == n_a - 1:
            w_f = jnp.pad(w_kvf[:, 2 * att:], ((0, 0), (0, LANES - heads)))
            k, v, f = _kv_proj(h, kv_norm, bf(w_kvf[:, :att]).T, bf(w_kvf[:, att:2 * att]),
                               bf(w_f), k_norm, tiles)
            f_rows = f[:, :heads].reshape(batch, seq, heads).transpose(0, 2, 1)
            c = _forget_cumsum(f_rows.reshape(batch * heads, seq),
                               jnp.tile(b_f, batch).reshape(batch * heads, 1))
    return h.reshape(batch, seq, d)


def kernel(x, a_norm, a_w_in, a_v_norm, a_w_s, a_b_s, a_w_out, kv_norm, w_kvf, b_f, k_norm,
           b_norm, b_w_qg, q_norm, b_w_out, f_norm, f_w_up, f_conv_w, f_conv_b, f_w_down,
           final_norm):
    return _forward(x, a_norm, a_w_in, a_v_norm, a_w_s, a_b_s, a_w_out, kv_norm, w_kvf, b_f,
                    k_norm, b_norm, b_w_qg, q_norm, b_w_out, f_norm, f_w_up, f_conv_w, f_conv_b,
                    f_w_down, final_norm)
```

```python
import functools
from typing import NamedTuple

import jax
import jax.numpy as jnp
from jax import lax
from jax.experimental import pallas as pl
from jax.experimental.pallas import tpu as pltpu

F32 = jnp.float32
BF16 = jnp.bfloat16
EPS = 1e-6
LOG2E = 1.4426950408889634
SUBLANES = 8
LANES = 128
MXU_COLS = 256
V7X_VMEM_BYTES = 64 * 1024 * 1024


class Tiles(NamedTuple):
    proj_m: int = 1024
    proj_n: int = 512
    gate_m: int = 512
    ffn_m: int = 1024
    ffn_n: int = 512
    attn: int = 512
    attn_heads: int = 4


def _params(semantics, vmem_bytes):
    return pltpu.CompilerParams(dimension_semantics=semantics,
                                vmem_limit_bytes=min(vmem_bytes, V7X_VMEM_BYTES - (4 << 20)))


def _rmsnorm(x, gain):
    r = lax.rsqrt(jnp.mean(x * x, axis=-1, keepdims=True) + EPS)
    return (x * r) * gain


def _head_rmsnorm_store(o_ref, x, gain, head_dim, scale=None):
    for i in range(x.shape[1] // head_dim):
        xh = x[:, i * head_dim:(i + 1) * head_dim]
        y = _rmsnorm(xh, gain)
        if scale is not None:
            y = y * scale
        o_ref[:, i * head_dim:(i + 1) * head_dim] = y.astype(o_ref.dtype)


def _gelu(x):
    return 0.5 * x * (1.0 + lax.erf(x * (0.5 ** 0.5)))


def _log_sigmoid(x):
    return jnp.minimum(x, 0.0) - jnp.log1p(jnp.exp(-jnp.abs(x)))


def _dot(a, b):
    return jnp.dot(a, b, preferred_element_type=F32)


def _store_xn(xn_ref, h_ref, g_ref):
    @pl.when(pl.program_id(1) == 0)
    def _():
        xn_ref[...] = _rmsnorm(h_ref[...], g_ref[...]).astype(BF16)


def _a_in_kernel(h_ref, g_ref, wu_ref, wv_ref, u_ref, v_ref, xn_ref):
    _store_xn(xn_ref, h_ref, g_ref)
    xn = xn_ref[...]
    u_ref[...] = _gelu(_dot(xn, wu_ref[...])).astype(BF16)
    v_ref[...] = _gelu(_dot(xn, wv_ref[...])).astype(BF16)


def _a_in(h, gain, w_in, layer, tiles):
    t, d = h.shape
    width = w_in.shape[2] // 2
    tm, tn = min(tiles.proj_m, t), min(tiles.proj_n, width)
    nn = width // tn
    vmem = 2 * tm * d * 4 + tm * d * 2 + 2 * 2 * d * tn * 2 + 2 * 2 * tm * tn * 2 + 6 * tm * tn * 4
    return pl.pallas_call(
        _a_in_kernel,
        grid=(t // tm, nn),
        in_specs=[
            pl.BlockSpec((tm, d), lambda m, n: (m, 0)),
            pl.BlockSpec((1, d), lambda m, n: (0, 0)),
            pl.BlockSpec((None, d, tn), lambda m, n: (layer, 0, n)),
            pl.BlockSpec((None, d, tn), lambda m, n: (layer, 0, n + nn)),
        ],
        out_specs=[pl.BlockSpec((tm, tn), lambda m, n: (m, n))] * 2,
        out_shape=[jax.ShapeDtypeStruct((t, width), BF16)] * 2,
        scratch_shapes=[pltpu.VMEM((tm, d), BF16)],
        compiler_params=_params(("parallel", "arbitrary"), vmem + (8 << 20)),
        name="a_in_proj",
    )(h, gain.reshape(1, d), w_in, w_in)


def _a_gate_kernel(h_ref, u_ref, v_ref, gn_ref, ws_ref, bt_ref, wo_ref, o_ref, gated_ref,
                   *, chunk, groups):
    tm, width = v_ref.shape
    gdim = width // groups
    row = lax.broadcasted_iota(jnp.int32, (chunk, chunk), 0)
    col = lax.broadcasted_iota(jnp.int32, (chunk, chunk), 1)
    causal = row >= col
    parts = 2 if (tm // chunk) % 2 == 0 else 1
    nchunks = tm // chunk // parts
    for part in range(parts):
        base = part * nchunks * chunk
        part_rows = slice(base, base + nchunks * chunk)
        vs = _rmsnorm(v_ref[part_rows, :].astype(F32), gn_ref[...]).astype(BF16)
        for g in range(groups):
            w = jnp.where(causal, ws_ref[g], jnp.zeros((), BF16))
            bias = bt_ref[:, g:g + 1]
            cols = slice(g * gdim, (g + 1) * gdim)
            rhs = jnp.concatenate([vs[c * chunk:(c + 1) * chunk, cols] for c in range(nchunks)],
                                  axis=1)
            mixed = _dot(w, rhs) + bias
            for c in range(nchunks):
                rows = slice(base + c * chunk, base + (c + 1) * chunk)
                gated_ref[rows, cols] = (u_ref[rows, cols].astype(F32)
                                         * mixed[:, c * gdim:(c + 1) * gdim]).astype(BF16)
        o_ref[part_rows, :] = h_ref[part_rows, :] + _dot(gated_ref[part_rows, :], wo_ref[...])


def _a_gate(h, u, v, v_gain, w_s, b_s, w_out, layer, tiles):
    t, d = h.shape
    width = u.shape[1]
    _, groups, chunk, _ = w_s.shape
    tm = min(tiles.gate_m, t)
    vmem = (4 * tm * d * 4 + 4 * tm * width * 2 + 2 * width * d * 2 + tm * width * 2
            + 3 * tm * width * 4 + tm * d * 4)
    return pl.pallas_call(
        functools.partial(_a_gate_kernel, chunk=chunk, groups=groups),
        grid=(t // tm,),
        in_specs=[
            pl.BlockSpec((tm, d), lambda m: (m, 0)),
            pl.BlockSpec((tm, width), lambda m: (m, 0)),
            pl.BlockSpec((tm, width), lambda m: (m, 0)),
            pl.BlockSpec((1, width), lambda m: (0, 0)),
            pl.BlockSpec((None, groups, chunk, chunk), lambda m: (layer, 0, 0, 0)),
            pl.BlockSpec((chunk, groups), lambda m: (0, 0)),
            pl.BlockSpec((None, width, d), lambda m: (layer, 0, 0), pipeline_mode=pl.Buffered(1)),
        ],
        out_specs=pl.BlockSpec((tm, d), lambda m: (m, 0)),
        out_shape=jax.ShapeDtypeStruct((t, d), F32),
        scratch_shapes=[pltpu.VMEM((tm, width), BF16)],
        compiler_params=_params(("parallel",), vmem + (8 << 20)),
        name="a_gate_out_proj",
    )(h, u, v, v_gain.reshape(1, width), w_s, b_s.T, w_out)


def _ffn_kernel(*refs, tiles_per_seq, nj, final):
    if final:
        (h_ref, g_ref, wg_ref, wv_ref, cwg_ref, cwv_ref, cbg_ref, cbv_ref, wd_ref, fg_ref,
         o_ref, xn_ref, acc_ref, carry_ref, act0_ref, act1_ref, upg_ref, upv_ref) = refs
    else:
        (h_ref, g_ref, wg_ref, wv_ref, cwg_ref, cwv_ref, cbg_ref, cbv_ref, wd_ref,
         o_ref, xn_ref, acc_ref, carry_ref, act0_ref, act1_ref, upg_ref, upv_ref) = refs
    acts = (act0_ref, act1_ref)
    m, j = pl.program_id(0), pl.program_id(1)
    tm, d = h_ref.shape
    tn = wg_ref.shape[1]
    seq_start = (m % tiles_per_seq) == 0

    @pl.when(j == 0)
    def _():
        xn_ref[...] = _rmsnorm(h_ref[...], g_ref[...]).astype(BF16)
        acc_ref[...] = jnp.zeros(acc_ref.shape, F32)

    @pl.when(jnp.logical_and(seq_start, j < nj))
    def _():
        carry_ref[2 * j] = jnp.zeros(carry_ref.shape[1:], F32)
        carry_ref[2 * j + 1] = jnp.zeros(carry_ref.shape[1:], F32)

    def up_dots():
        return _dot(xn_ref[...], wg_ref[...]), _dot(xn_ref[...], wv_ref[...])

    def conv(up, cw_ref, cb_ref, slot, up_ref):
        up_ref[:SUBLANES] = carry_ref[slot]
        up_ref[SUBLANES:] = up
        carry_ref[slot] = up[tm - SUBLANES:, :]
        w0, w1, w2, b = cw_ref[0:1], cw_ref[1:2], cw_ref[2:3], cb_ref[...]
        return (up * w2 + up_ref[SUBLANES - 1:SUBLANES - 1 + tm] * w1
                + up_ref[SUBLANES - 2:SUBLANES - 2 + tm] * w0 + b)

    def gate_act(ups, act_ref):
        gate = conv(ups[0], cwg_ref, cbg_ref, 2 * j, upg_ref)
        val = conv(ups[1], cwv_ref, cbv_ref, 2 * j + 1, upv_ref)
        act_ref[...] = (jax.nn.silu(gate) * val).astype(BF16)

    def down_proj(act_ref):
        acc_ref[...] += _dot(act_ref[...], wd_ref[...])

    @pl.when(j == 0)
    def _():
        gate_act(up_dots(), acts[0])

    for parity in range(2):
        @pl.when(jnp.logical_and(jnp.logical_and(j > 0, j < nj), j % 2 == parity))
        def _():
            ups = up_dots()
            down_proj(acts[1 - parity])
            gate_act(ups, acts[parity])

    @pl.when(j == nj)
    def _():
        down_proj(acts[(nj - 1) % 2])
        y = h_ref[...] + acc_ref[...]
        if final:
            y = _rmsnorm(y, fg_ref[...])
        o_ref[...] = y


def _ffn(h, gain, w_up, conv_w, conv_b, w_down, layer, seq_len, tiles, final_gain=None):
    t, d = h.shape
    ffn = w_down.shape[1]
    tm, tn = min(tiles.ffn_m, seq_len), min(tiles.ffn_n, ffn)
    nj = ffn // tn
    assert seq_len % tm == 0 and ffn % tn == 0
    final = final_gain is not None
    row = lambda m, j: (m, 0)
    const = lambda m, j: (0, 0)
    gate_tile = lambda m, j: (layer, 0, jnp.minimum(j, nj - 1))
    val_tile = lambda m, j: (layer, 0, jnp.minimum(j, nj - 1) + nj)
    in_specs = [
        pl.BlockSpec((tm, d), row, pipeline_mode=pl.Buffered(1)),
        pl.BlockSpec((1, d), const),
        pl.BlockSpec((None, d, tn), gate_tile),
        pl.BlockSpec((None, d, tn), val_tile),
        pl.BlockSpec((None, conv_w.shape[1], tn), gate_tile),
        pl.BlockSpec((None, conv_w.shape[1], tn), val_tile),
        pl.BlockSpec((None, 1, tn), gate_tile),
        pl.BlockSpec((None, 1, tn), val_tile),
        pl.BlockSpec((None, tn, d), lambda m, j: (layer, jnp.maximum(j - 1, 0), 0)),
    ]
    conv_b = conv_b.reshape(conv_b.shape[0], 1, -1)
    args = [h, gain.reshape(1, d), w_up, w_up, conv_w, conv_w, conv_b, conv_b, w_down]
    if final:
        in_specs.append(pl.BlockSpec((1, d), const))
        args.append(final_gain.reshape(1, d))
    vmem = (2 * tm * d * 4 + tm * d * 2 + tm * d * 4 + 2 * 3 * d * tn * 2 + 2 * tm * tn * 2
            + 8 * tm * tn * 4)
    return pl.pallas_call(
        functools.partial(_ffn_kernel, tiles_per_seq=seq_len // tm, nj=nj, final=final),
        grid=(t // tm, nj + 1),
        in_specs=in_specs,
        out_specs=pl.BlockSpec((tm, d), row, pipeline_mode=pl.Buffered(1)),
        out_shape=jax.ShapeDtypeStruct((t, d), F32),
        scratch_shapes=[
            pltpu.VMEM((tm, d), BF16),
            pltpu.VMEM((tm, d), F32),
            pltpu.VMEM((2 * nj, SUBLANES, tn), F32),
            pltpu.VMEM((tm, tn), BF16),
            pltpu.VMEM((tm, tn), BF16),
            pltpu.VMEM((SUBLANES + tm, tn), F32),
            pltpu.VMEM((SUBLANES + tm, tn), F32),
        ],
        compiler_params=_params(("arbitrary", "arbitrary"), vmem + (8 << 20)),
        name="conv_ffn",
    )(*args)


def _kv_kernel(h_ref, g_ref, wkt_ref, wv_ref, wf_ref, kn_ref, kt_ref, v_ref, f_ref, xn_ref,
               *, head_dim):
    _store_xn(xn_ref, h_ref, g_ref)
    xn = xn_ref[...]
    kt = lax.dot_general(wkt_ref[...], xn, (((1,), (1,)), ((), ())), preferred_element_type=F32)
    for i in range(kt.shape[0] // head_dim):
        rows = slice(i * head_dim, (i + 1) * head_dim)
        x = kt[rows]
        r = lax.rsqrt(jnp.mean(x * x, axis=0, keepdims=True) + EPS)
        kt_ref[rows, :] = ((x * r) * kn_ref[...]).astype(BF16)
    v_ref[...] = _dot(xn, wv_ref[...]).astype(BF16)

    @pl.when(pl.program_id(1) == 0)
    def _():
        f_ref[...] = _dot(xn, wf_ref[...])


def _kv_proj(h, gain, w_kt, w_v, w_f, k_gain, tiles):
    t, d = h.shape
    width = w_v.shape[1]
    head_dim = k_gain.shape[0]
    tm, tn = min(tiles.proj_m, t), min(tiles.proj_n, width)
    vmem = (2 * tm * d * 4 + tm * d * 2 + 2 * 2 * d * tn * 2 + 2 * d * LANES * 2
            + 2 * 2 * tm * tn * 2 + 2 * tm * LANES * 4 + 6 * tm * tn * 4)
    return pl.pallas_call(
        functools.partial(_kv_kernel, head_dim=head_dim),
        grid=(t // tm, width // tn),
        in_specs=[
            pl.BlockSpec((tm, d), lambda m, n: (m, 0)),
            pl.BlockSpec((1, d), lambda m, n: (0, 0)),
            pl.BlockSpec((tn, d), lambda m, n: (n, 0)),
            pl.BlockSpec((d, tn), lambda m, n: (0, n)),
            pl.BlockSpec((d, LANES), lambda m, n: (0, 0)),
            pl.BlockSpec((head_dim, 1), lambda m, n: (0, 0)),
        ],
        out_specs=[
            pl.BlockSpec((tn, tm), lambda m, n: (n, m)),
            pl.BlockSpec((tm, tn), lambda m, n: (m, n)),
            pl.BlockSpec((tm, LANES), lambda m, n: (m, 0)),
        ],
        out_shape=[
            jax.ShapeDtypeStruct((width, t), BF16),
            jax.ShapeDtypeStruct((t, width), BF16),
            jax.ShapeDtypeStruct((t, LANES), F32),
        ],
        scratch_shapes=[pltpu.VMEM((tm, d), BF16)],
        compiler_params=_params(("parallel", "arbitrary"), vmem + (8 << 20)),
        name="kv_proj",
    )(h, gain.reshape(1, d), w_kt, w_v, w_f, k_gain.reshape(head_dim, 1))


def _forget_cumsum_kernel(f_ref, b_ref, c_ref):
    rows, seq = f_ref.shape
    lane = lax.broadcasted_iota(jnp.int32, (rows, LANES), 1)
    carry = jnp.zeros((rows, 1), F32)
    for blk in range(seq // LANES):
        cols = slice(blk * LANES, (blk + 1) * LANES)
        x = _log_sigmoid(f_ref[:, cols] + b_ref[...])
        d = 1
        while d < LANES:
            x = x + jnp.where(lane >= d, pltpu.roll(x, d, axis=1), 0.0)
            d *= 2
        x = x + carry
        c_ref[:, cols] = x
        carry = x[:, LANES - 1:LANES]


def _forget_cumsum(f_rows, b_rows):
    rows, seq = f_rows.shape
    return pl.pallas_call(
        _forget_cumsum_kernel,
        out_shape=jax.ShapeDtypeStruct((rows, seq), F32),
        name="forget_cumsum",
    )(f_rows, b_rows)


def _qg_kernel(h_ref, g_ref, wq_ref, wg_ref, qn_ref, q_ref, sg_ref, xn_ref, *, head_dim):
    _store_xn(xn_ref, h_ref, g_ref)
    xn = xn_ref[...]
    _head_rmsnorm_store(q_ref, _dot(xn, wq_ref[...]), qn_ref[...], head_dim,
                        scale=head_dim ** -0.5 * LOG2E)
    sg_ref[...] = jax.nn.sigmoid(_dot(xn, wg_ref[...])).astype(BF16)


def _qg_proj(h, gain, w_qg, q_gain, layer, tiles):
    t, d = h.shape
    width = w_qg.shape[2] // 2
    head_dim = q_gain.shape[0]
    tm, tn = min(tiles.proj_m, t), min(tiles.proj_n, width)
    nn = width // tn
    vmem = 2 * tm * d * 4 + tm * d * 2 + 2 * 2 * d * tn * 2 + 2 * 2 * tm * tn * 2 + 6 * tm * tn * 4
    return pl.pallas_call(
        functools.partial(_qg_kernel, head_dim=head_dim),
        grid=(t // tm, nn),
        in_specs=[
            pl.BlockSpec((tm, d), lambda m, n: (m, 0)),
            pl.BlockSpec((1, d), lambda m, n: (0, 0)),
            pl.BlockSpec((None, d, tn), lambda m, n: (layer, 0, n)),
            pl.BlockSpec((None, d, tn), lambda m, n: (layer, 0, n + nn)),
            pl.BlockSpec((1, head_dim), lambda m, n: (0, 0)),
        ],
        out_specs=[pl.BlockSpec((tm, tn), lambda m, n: (m, n))] * 2,
        out_shape=[jax.ShapeDtypeStruct((t, width), BF16)] * 2,
        scratch_shapes=[pltpu.VMEM((tm, d), BF16)],
        compiler_params=_params(("parallel", "arbitrary"), vmem + (8 << 20)),
        name="qg_proj",
    )(h, gain.reshape(1, d), w_qg, w_qg, q_gain.reshape(1, head_dim))


C_TERMS = 3


def _split_bf16_terms(x):
    terms = []
    for _ in range(C_TERMS):
        t = x.astype(BF16).astype(F32)
        terms.append(t)
        x = x - t
    return terms


def _attn_kernel(q_ref, kt_ref, v_ref, crow_ref, sg_ref, o_ref,
                 qa_ref, kta_ref, va_ref, m_ref, acc_ref, *, heads, head_dim):
    blk = q_ref.shape[0]
    nk = kta_ref.shape[1]
    hd = head_dim
    qi = pl.program_id(2)

    @pl.when(qi == 0)
    def _():
        sub = lax.broadcasted_iota(jnp.int32, (2 * SUBLANES, blk), 0)
        for h in range(heads):
            for kb in range(nk):
                cols = slice(kb * blk, (kb + 1) * blk)
                c_s = _split_bf16_terms(crow_ref[h, kb] * LOG2E)
                tail = jnp.where(sub < C_TERMS, 1.0, 0.0)
                for j, term in enumerate(c_s):
                    tail = jnp.where(sub == C_TERMS + j, -term, tail)
                kta_ref[h, kb, :hd, :] = kt_ref[h * hd:(h + 1) * hd, cols]
                kta_ref[h, kb, hd:hd + 2 * SUBLANES, :] = tail.astype(BF16)
                kta_ref[h, kb, hd + 2 * SUBLANES:, :] = jnp.zeros((hd - 2 * SUBLANES, blk), BF16)
            va_ref[h, :, :hd] = v_ref[:, h * hd:(h + 1) * hd]
            va_ref[h, :, hd:] = jnp.ones((v_ref.shape[0], hd), BF16)

    sub = lax.broadcasted_iota(jnp.int32, (hd, hd), 0)
    for h in range(heads):
        c_t = _split_bf16_terms(crow_ref[h, qi] * LOG2E)
        qa_ref[h, :, :hd] = q_ref[:, h * hd:(h + 1) * hd]
        for rb in range(blk // hd):
            rows = slice(rb * hd, (rb + 1) * hd)
            tail_t = jnp.where(jnp.logical_and(sub >= C_TERMS, sub < 2 * C_TERMS), 1.0, 0.0)
            for j, term in enumerate(c_t):
                tail_t = jnp.where(sub == j, term[:, rows], tail_t)
            qa_ref[h, rows, hd:] = tail_t.T.astype(BF16)

    def blocks(k0, count, diagonal):
        start = pl.multiple_of(k0 * blk, blk)
        scores = []
        for h in range(heads):
            chunks = []
            for b in range(count):
                s = _dot(qa_ref[h], kta_ref[h, k0 + b])
                if diagonal:
                    row = lax.broadcasted_iota(jnp.int32, (blk, blk), 0)
                    col = lax.broadcasted_iota(jnp.int32, (blk, blk), 1)
                    s = jnp.where(row >= col, s, -jnp.inf)
                chunks += [s[:, c * LANES:(c + 1) * LANES] for c in range(blk // LANES)]
            scores.append(chunks)
        probs = []
        for h, chunks in enumerate(scores):
            m_new = jnp.max(functools.reduce(jnp.maximum, chunks), axis=1, keepdims=True)
            if diagonal:
                m_new = jnp.broadcast_to(m_new, (blk, LANES))
                alpha = None
            else:
                m_old = m_ref[h]
                m_new = jnp.maximum(m_old, m_new)
                alpha = jnp.exp2(m_old - m_new)
            m_ref[h] = m_new
            p = jnp.concatenate([jnp.exp2(c - m_new).astype(BF16) for c in chunks], axis=1)
            probs.append((p, alpha))
        for h, (p, alpha) in enumerate(probs):
            pv = _dot(p, va_ref[h, pl.ds(start, count * blk), :])
            if diagonal:
                acc_ref[h] = pv
            else:
                acc_ref[h] = jnp.concatenate([alpha] * (2 * hd // LANES), axis=1) * acc_ref[h] + pv

    blocks(qi, 1, True)
    odd = qi % 2

    @pl.when(odd == 1)
    def _():
        blocks(0, 1, False)

    def pair(i, carry):
        blocks(2 * i + odd, 2, False)
        return carry

    lax.fori_loop(0, qi // 2, pair, 0)
    for h in range(heads):
        acc = acc_ref[h]
        o = acc[:, :hd] / acc[:, hd:]
        cols = slice(h * hd, (h + 1) * hd)
        o_ref[:, cols] = (o * sg_ref[:, cols].astype(F32)).astype(o_ref.dtype)


def _attention(q, kt, v, c, sg, batch, heads, tiles):
    t, width = q.shape
    seq = t // batch
    hd = width // heads
    assert hd == LANES
    blk = min(tiles.attn, seq)
    nq = seq // blk
    hp = min(tiles.attn_heads, heads)
    c_row = c.reshape(batch * heads, nq, 1, blk)
    qmap = lambda b, g, i: (b * nq + i, g)
    vmem = (2 * 2 * seq * hp * hd * 2 + 2 * seq * hp * 2 * hd * 2 + 2 * 3 * blk * hp * hd * 2
            + hp * blk * (2 * hd * 2 + LANES * 4 + 2 * hd * 4) + 2 * hp * seq * 4 * SUBLANES
            + 2 * hp * blk * LANES * 4 + hp * 3 * blk * blk * 4)
    return pl.pallas_call(
        functools.partial(_attn_kernel, heads=hp, head_dim=hd),
        grid=(batch, heads // hp, nq),
        in_specs=[
            pl.BlockSpec((blk, hp * hd), qmap),
            pl.BlockSpec((hp * hd, seq), lambda b, g, i: (g, b)),
            pl.BlockSpec((seq, hp * hd), lambda b, g, i: (b, g)),
            pl.BlockSpec((hp, nq, 1, blk), lambda b, g, i: (b * (heads // hp) + g, 0, 0, 0)),
            pl.BlockSpec((blk, hp * hd), qmap),
        ],
        out_specs=pl.BlockSpec((blk, hp * hd), qmap),
        out_shape=jax.ShapeDtypeStruct((t, width), BF16),
        scratch_shapes=[
            pltpu.VMEM((hp, blk, 2 * hd), BF16),
            pltpu.VMEM((hp, nq, 2 * hd, blk), BF16),
            pltpu.VMEM((hp, seq, 2 * hd), BF16),
            pltpu.VMEM((hp, blk, LANES), F32),
            pltpu.VMEM((hp, blk, 2 * hd), F32),
        ],
        compiler_params=_params(("parallel", "parallel", "arbitrary"), vmem + (8 << 20)),
        name="fox_attention",
    )(q, kt, v, c_row, sg)


def _out_proj_kernel(h_ref, a_ref, w_ref, o_ref):
    o_ref[...] = h_ref[...] + _dot(a_ref[...], w_ref[...])


def _out_proj(h, a, w, layer, tiles):
    t, d = h.shape
    kdim = a.shape[1]
    tm = min(tiles.gate_m, t)
    vmem = 4 * tm * d * 4 + 2 * tm * kdim * 2 + kdim * d * 2 + 2 * tm * d * 4
    return pl.pallas_call(
        _out_proj_kernel,
        grid=(t // tm,),
        in_specs=[
            pl.BlockSpec((tm, d), lambda m: (m, 0)),
            pl.BlockSpec((tm, kdim), lambda m: (m, 0)),
            pl.BlockSpec((None, kdim, d), lambda m: (layer, 0, 0), pipeline_mode=pl.Buffered(1)),
        ],
        out_specs=pl.BlockSpec((tm, d), lambda m: (m, 0)),
        out_shape=jax.ShapeDtypeStruct((t, d), F32),
        compiler_params=_params(("parallel",), vmem + (8 << 20)),
        name="b_out_proj",
    )(h, a, w)


def _forward(x, a_norm, a_w_in, a_v_norm, a_w_s, a_b_s, a_w_out, kv_norm, w_kvf, b_f, k_norm,
             b_norm, b_w_qg, q_norm, b_w_out, f_norm, f_w_up, f_conv_w, f_conv_b, f_w_down,
             final_norm, tiles=Tiles()):
    batch, seq, d = x.shape
    depth = f_norm.shape[0]
    n_a = a_norm.shape[0]
    heads = b_f.shape[0]
    att = heads * k_norm.shape[0]
    h = x.reshape(batch * seq, d)
    bf = lambda w: w.astype(BF16)
    a_w_in, a_w_s, a_w_out = bf(a_w_in), bf(a_w_s), bf(a_w_out)
    b_w_qg, b_w_out = bf(b_w_qg), bf(b_w_out)
    f_w_up, f_w_down = bf(f_w_up), bf(f_w_down)
    k = v = c = None
    for l in range(depth):
        if l < n_a:
            u, vv = _a_in(h, a_norm[l], a_w_in, l, tiles)
            h = _a_gate(h, u, vv, a_v_norm[l], a_w_s, a_b_s[l], a_w_out, l, tiles)
        else:
            j = l - n_a
            q, sg = _qg_proj(h, b_norm[j], b_w_qg, q_norm[j], j, tiles)
            og = _attention(q, k, v, c, sg, batch, heads, tiles)
            h = _out_proj(h, og, b_w_out, j, tiles)
        h = _ffn(h, f_norm[l], f_w_up, f_conv_w, f_conv_b, f_w_down, l, seq, tiles,
                 final_gain=final_norm if l == depth - 1 else None)
        if l == n_a - 1:
            w_f = jnp.pad(w_kvf[:, 2 * att:], ((0, 0), (0, LANES - heads)))
            k, v, f = _kv_proj(h, kv_norm, bf(w_kvf[:, :att]).T, bf(w_kvf[:, att:2 * att]),
                               bf(w_f), k_norm, tiles)
            f_rows = f[:, :heads].reshape(batch, seq, heads).transpose(0, 2, 1)
            c = _forget_cumsum(f_rows.reshape(batch * heads, seq),
                               jnp.tile(b_f, batch).reshape(batch * heads, 1))
    return h.reshape(batch, seq, d)


def kernel(x, a_norm, a_w_in, a_v_norm, a_w_s, a_b_s, a_w_out, kv_norm, w_kvf, b_f, k_norm,
           b_norm, b_w_qg, q_norm, b_w_out, f_norm, f_w_up, f_conv_w, f_conv_b, f_w_down,
           final_norm):
    return _forward(x, a_norm, a_w_in, a_v_norm, a_w_s, a_b_s, a_w_out, kv_norm, w_kvf, b_f,
                    k_norm, b_norm, b_w_qg, q_norm, b_w_out, f_norm, f_w_up, f_conv_w, f_conv_b,
                    f_w_down, final_norm)
```

```python
import functools
from typing import NamedTuple

import jax
import jax.numpy as jnp
from jax import lax
from jax.experimental import pallas as pl
from jax.experimental.pallas import tpu as pltpu

F32 = jnp.float32
BF16 = jnp.bfloat16
EPS = 1e-6
LOG2E = 1.4426950408889634
SUBLANES = 8
LANES = 128
MXU_COLS = 256
V7X_VMEM_BYTES = 64 * 1024 * 1024


class Tiles(NamedTuple):
    proj_m: int = 1024
    proj_n: int = 512
    gate_m: int = 512
    ffn_m: int = 1024
    ffn_n: int = 512
    attn: int = 512
    attn_heads: int = 4


def _params(semantics, vmem_bytes):
    return pltpu.CompilerParams(dimension_semantics=semantics,
                                vmem_limit_bytes=min(vmem_bytes, V7X_VMEM_BYTES - (4 << 20)))


def _rmsnorm(x, gain):
    r = lax.rsqrt(jnp.mean(x * x, axis=-1, keepdims=True) + EPS)
    return (x * r) * gain


def _head_rmsnorm_store(o_ref, x, gain, head_dim, scale=None):
    for i in range(x.shape[1] // head_dim):
        xh = x[:, i * head_dim:(i + 1) * head_dim]
        y = _rmsnorm(xh, gain)
        if scale is not None:
            y = y * scale
        o_ref[:, i * head_dim:(i + 1) * head_dim] = y.astype(o_ref.dtype)


def _gelu(x):
    return 0.5 * x * (1.0 + lax.erf(x * (0.5 ** 0.5)))


def _log_sigmoid(x):
    return jnp.minimum(x, 0.0) - jnp.log1p(jnp.exp(-jnp.abs(x)))


def _dot(a, b):
    return jnp.dot(a, b, preferred_element_type=F32)


def _wdot(x, w_ref):
    return _dot(x, w_ref[...].astype(BF16))


def _store_xn(xn_ref, h_ref, g_ref):
    @pl.when(pl.program_id(1) == 0)
    def _():
        xn_ref[...] = _rmsnorm(h_ref[...], g_ref[...]).astype(BF16)


def _a_in_kernel(h_ref, g_ref, wu_ref, wv_ref, u_ref, v_ref, xn_ref):
    _store_xn(xn_ref, h_ref, g_ref)
    xn = xn_ref[...]
    u_ref[...] = _gelu(_wdot(xn, wu_ref)).astype(BF16)
    v_ref[...] = _gelu(_wdot(xn, wv_ref)).astype(BF16)


def _a_in(h, gain, w_in, layer, tiles):
    t, d = h.shape
    width = w_in.shape[2] // 2
    tm, tn = min(tiles.proj_m, t), min(tiles.proj_n, width)
    nn = width // tn
    vmem = 2 * tm * d * 4 + tm * d * 2 + 2 * 2 * d * tn * 6 + 2 * 2 * tm * tn * 2 + 6 * tm * tn * 4
    return pl.pallas_call(
        _a_in_kernel,
        grid=(t // tm, nn),
        in_specs=[
            pl.BlockSpec((tm, d), lambda m, n: (m, 0)),
            pl.BlockSpec((1, d), lambda m, n: (0, 0)),
            pl.BlockSpec((None, d, tn), lambda m, n: (layer, 0, n)),
            pl.BlockSpec((None, d, tn), lambda m, n: (layer, 0, n + nn)),
        ],
        out_specs=[pl.BlockSpec((tm, tn), lambda m, n: (m, n))] * 2,
        out_shape=[jax.ShapeDtypeStruct((t, width), BF16)] * 2,
        scratch_shapes=[pltpu.VMEM((tm, d), BF16)],
        compiler_params=_params(("parallel", "arbitrary"), vmem + (8 << 20)),
        name="a_in_proj",
    )(h, gain.reshape(1, d), w_in, w_in)


def _a_gate_kernel(h_ref, u_ref, v_ref, gn_ref, ws_ref, bt_ref, wo_ref, o_ref, gated_ref,
                   *, chunk, groups):
    tm, width = v_ref.shape
    gdim = width // groups
    row = lax.broadcasted_iota(jnp.int32, (chunk, chunk), 0)
    col = lax.broadcasted_iota(jnp.int32, (chunk, chunk), 1)
    causal = row >= col
    parts = 2 if (tm // chunk) % 2 == 0 else 1
    nchunks = tm // chunk // parts
    for part in range(parts):
        base = part * nchunks * chunk
        part_rows = slice(base, base + nchunks * chunk)
        vs = _rmsnorm(v_ref[part_rows, :].astype(F32), gn_ref[...]).astype(BF16)
        for g in range(groups):
            w = jnp.where(causal, ws_ref[g], jnp.zeros((), BF16))
            bias = bt_ref[:, g:g + 1]
            cols = slice(g * gdim, (g + 1) * gdim)
            rhs = jnp.concatenate([vs[c * chunk:(c + 1) * chunk, cols] for c in range(nchunks)],
                                  axis=1)
            mixed = _dot(w, rhs) + bias
            for c in range(nchunks):
                rows = slice(base + c * chunk, base + (c + 1) * chunk)
                gated_ref[rows, cols] = (u_ref[rows, cols].astype(F32)
                                         * mixed[:, c * gdim:(c + 1) * gdim]).astype(BF16)
        o_ref[part_rows, :] = h_ref[part_rows, :] + _dot(gated_ref[part_rows, :], wo_ref[...])


def _a_gate(h, u, v, v_gain, w_s, b_s, w_out, layer, tiles):
    t, d = h.shape
    width = u.shape[1]
    _, groups, chunk, _ = w_s.shape
    tm = min(tiles.gate_m, t)
    vmem = (4 * tm * d * 4 + 4 * tm * width * 2 + 2 * width * d * 2 + tm * width * 2
            + 3 * tm * width * 4 + tm * d * 4)
    return pl.pallas_call(
        functools.partial(_a_gate_kernel, chunk=chunk, groups=groups),
        grid=(t // tm,),
        in_specs=[
            pl.BlockSpec((tm, d), lambda m: (m, 0)),
            pl.BlockSpec((tm, width), lambda m: (m, 0)),
            pl.BlockSpec((tm, width), lambda m: (m, 0)),
            pl.BlockSpec((1, width), lambda m: (0, 0)),
            pl.BlockSpec((None, groups, chunk, chunk), lambda m: (layer, 0, 0, 0)),
            pl.BlockSpec((chunk, groups), lambda m: (0, 0)),
            pl.BlockSpec((None, width, d), lambda m: (layer, 0, 0), pipeline_mode=pl.Buffered(1)),
        ],
        out_specs=pl.BlockSpec((tm, d), lambda m: (m, 0)),
        out_shape=jax.ShapeDtypeStruct((t, d), F32),
        scratch_shapes=[pltpu.VMEM((tm, width), BF16)],
        compiler_params=_params(("parallel",), vmem + (8 << 20)),
        name="a_gate_out_proj",
    )(h, u, v, v_gain.reshape(1, width), w_s, b_s.T, w_out)


def _ffn_kernel(*refs, tiles_per_seq, nj, final, cast_next):
    (h_ref, g_ref, wg_ref, wv_ref, cwg_ref, cwv_ref, cbg_ref, cbv_ref, wd_ref), refs = refs[:9], refs[9:]
    if final:
        fg_ref, refs = refs[0], refs[1:]
    if cast_next:
        (next_up_ref, next_down_ref), refs = refs[:2], refs[2:]
    o_ref, refs = refs[0], refs[1:]
    if cast_next:
        (next_up_out, next_down_out), refs = refs[:2], refs[2:]
    xn_ref, acc_ref, carry_ref, act0_ref, act1_ref, upg_ref, upv_ref = refs

    def cast_side_stream():
        if cast_next:
            next_up_out[...] = next_up_ref[...].astype(BF16)
            next_down_out[...] = next_down_ref[...].astype(BF16)
    acts = (act0_ref, act1_ref)
    m, j = pl.program_id(0), pl.program_id(1)
    tm, d = h_ref.shape
    tn = wg_ref.shape[1]
    seq_start = (m % tiles_per_seq) == 0

    @pl.when(j == 0)
    def _():
        xn_ref[...] = _rmsnorm(h_ref[...], g_ref[...]).astype(BF16)
        acc_ref[...] = jnp.zeros(acc_ref.shape, F32)

    @pl.when(jnp.logical_and(seq_start, j < nj))
    def _():
        carry_ref[2 * j] = jnp.zeros(carry_ref.shape[1:], F32)
        carry_ref[2 * j + 1] = jnp.zeros(carry_ref.shape[1:], F32)

    def up_dots():
        return _dot(xn_ref[...], wg_ref[...]), _dot(xn_ref[...], wv_ref[...])

    def conv(up, cw_ref, cb_ref, slot, up_ref):
        up_ref[:SUBLANES] = carry_ref[slot]
        up_ref[SUBLANES:] = up
        carry_ref[slot] = up[tm - SUBLANES:, :]
        w0, w1, w2, b = cw_ref[0:1], cw_ref[1:2], cw_ref[2:3], cb_ref[...]
        return (up * w2 + up_ref[SUBLANES - 1:SUBLANES - 1 + tm] * w1
                + up_ref[SUBLANES - 2:SUBLANES - 2 + tm] * w0 + b)

    def gate_act(ups, act_ref):
        gate = conv(ups[0], cwg_ref, cbg_ref, 2 * j, upg_ref)
        val = conv(ups[1], cwv_ref, cbv_ref, 2 * j + 1, upv_ref)
        act_ref[...] = (jax.nn.silu(gate) * val).astype(BF16)

    def down_proj(act_ref):
        acc_ref[...] += _dot(act_ref[...], wd_ref[...])

    @pl.when(j == 0)
    def _():
        cast_side_stream()
        gate_act(up_dots(), acts[0])

    for parity in range(2):
        @pl.when(jnp.logical_and(jnp.logical_and(j > 0, j < nj), j % 2 == parity))
        def _():
            cast_side_stream()
            ups = up_dots()
            down_proj(acts[1 - parity])
            gate_act(ups, acts[parity])

    @pl.when(j == nj)
    def _():
        cast_side_stream()
        down_proj(acts[(nj - 1) % 2])
        y = h_ref[...] + acc_ref[...]
        if final:
            y = _rmsnorm(y, fg_ref[...])
        o_ref[...] = y


def _ffn_tiles(t, seq_len, ffn, tiles):
    tm, tn = min(tiles.ffn_m, seq_len), min(tiles.ffn_n, ffn)
    assert seq_len % tm == 0 and ffn % tn == 0
    return tm, tn, t // tm, ffn // tn


def _can_cast_in_ffn(t, d, seq_len, ffn, tiles):
    _, _, nm, nj = _ffn_tiles(t, seq_len, ffn, tiles)
    return (d % nm == 0 and (d // nm) % LANES == 0 and (2 * ffn) % nj == 0
            and (2 * ffn // nj) % LANES == 0 and (ffn // nj) % (2 * SUBLANES) == 0)


def _ffn(h, gain, w_up, conv_w, conv_b, w_down, layer, seq_len, tiles, final_gain=None,
         next_weights=None):
    t, d = h.shape
    ffn = w_down.shape[0]
    tm, tn, nm, nj = _ffn_tiles(t, seq_len, ffn, tiles)
    final = final_gain is not None
    cast_next = next_weights is not None
    row = lambda m, j: (m, 0)
    const = lambda m, j: (0, 0)
    up_step = lambda j: jnp.minimum(j, nj - 1)
    in_specs = [
        pl.BlockSpec((tm, d), row, pipeline_mode=pl.Buffered(1)),
        pl.BlockSpec((1, d), const),
        pl.BlockSpec((d, tn), lambda m, j: (0, up_step(j))),
        pl.BlockSpec((d, tn), lambda m, j: (0, up_step(j) + nj)),
        pl.BlockSpec((None, conv_w.shape[1], tn), lambda m, j: (layer, 0, up_step(j))),
        pl.BlockSpec((None, conv_w.shape[1], tn), lambda m, j: (layer, 0, up_step(j) + nj)),
        pl.BlockSpec((None, 1, tn), lambda m, j: (layer, 0, up_step(j))),
        pl.BlockSpec((None, 1, tn), lambda m, j: (layer, 0, up_step(j) + nj)),
        pl.BlockSpec((tn, d), lambda m, j: (jnp.maximum(j - 1, 0), 0)),
    ]
    conv_b = conv_b.reshape(conv_b.shape[0], 1, -1)
    args = [h, gain.reshape(1, d), w_up, w_up, conv_w, conv_w, conv_b, conv_b, w_down]
    out_specs = [pl.BlockSpec((tm, d), row, pipeline_mode=pl.Buffered(1))]
    out_shape = [jax.ShapeDtypeStruct((t, d), F32)]
    if final:
        in_specs.append(pl.BlockSpec((1, d), const))
        args.append(final_gain.reshape(1, d))
    if cast_next:
        nxt_up, nxt_down, nxt = next_weights
        up_blk, down_blk = (d // nm, 2 * ffn // nj), (ffn // nj, d // nm)
        in_specs += [pl.BlockSpec((None,) + up_blk, lambda m, j: (nxt, m, up_step(j))),
                     pl.BlockSpec((None,) + down_blk, lambda m, j: (nxt, up_step(j), m))]
        out_specs += [pl.BlockSpec(up_blk, lambda m, j: (m, up_step(j))),
                      pl.BlockSpec(down_blk, lambda m, j: (up_step(j), m))]
        out_shape += [jax.ShapeDtypeStruct((d, 2 * ffn), BF16),
                      jax.ShapeDtypeStruct((ffn, d), BF16)]
        args += [nxt_up, nxt_down]
    vmem = (2 * tm * d * 4 + tm * d * 2 + tm * d * 4 + 2 * 3 * d * tn * 2 + 2 * tm * tn * 2
            + 8 * tm * tn * 4 + (2 * 6 * 3 * d * ffn // (nm * nj) if cast_next else 0))
    outs = pl.pallas_call(
        functools.partial(_ffn_kernel, tiles_per_seq=seq_len // tm, nj=nj, final=final,
                          cast_next=cast_next),
        grid=(nm, nj + 1),
        in_specs=in_specs,
        out_specs=out_specs,
        out_shape=out_shape,
        scratch_shapes=[
            pltpu.VMEM((tm, d), BF16),
            pltpu.VMEM((tm, d), F32),
            pltpu.VMEM((2 * nj, SUBLANES, tn), F32),
            pltpu.VMEM((tm, tn), BF16),
            pltpu.VMEM((tm, tn), BF16),
            pltpu.VMEM((SUBLANES + tm, tn), F32),
            pltpu.VMEM((SUBLANES + tm, tn), F32),
        ],
        compiler_params=_params(("arbitrary", "arbitrary"), vmem + (8 << 20)),
        name="conv_ffn",
    )(*args)
    return outs if cast_next else outs[0]


def _kv_kernel(h_ref, g_ref, wkt_ref, wv_ref, wf_ref, kn_ref, kt_ref, v_ref, f_ref, xn_ref,
               *, head_dim):
    _store_xn(xn_ref, h_ref, g_ref)
    xn = xn_ref[...]
    kt = lax.dot_general(wkt_ref[...].astype(BF16), xn, (((1,), (1,)), ((), ())),
                         preferred_element_type=F32)
    for i in range(kt.shape[0] // head_dim):
        rows = slice(i * head_dim, (i + 1) * head_dim)
        x = kt[rows]
        r = lax.rsqrt(jnp.mean(x * x, axis=0, keepdims=True) + EPS)
        kt_ref[rows, :] = ((x * r) * kn_ref[...]).astype(BF16)
    v_ref[...] = _wdot(xn, wv_ref).astype(BF16)

    @pl.when(pl.program_id(1) == 0)
    def _():
        f_ref[...] = _wdot(xn, wf_ref)


def _kv_proj(h, gain, w_kt, w_kvf, w_f, k_gain, tiles):
    t, d = h.shape
    width = w_kt.shape[0]
    head_dim = k_gain.shape[0]
    tm, tn = min(tiles.proj_m, t), min(tiles.proj_n, width)
    v_first = width // tn
    vmem = (2 * tm * d * 4 + tm * d * 2 + 2 * 2 * d * tn * 6 + 2 * d * LANES * 6
            + 2 * 2 * tm * tn * 2 + 2 * tm * LANES * 4 + 6 * tm * tn * 4)
    return pl.pallas_call(
        functools.partial(_kv_kernel, head_dim=head_dim),
        grid=(t // tm, width // tn),
        in_specs=[
            pl.BlockSpec((tm, d), lambda m, n: (m, 0)),
            pl.BlockSpec((1, d), lambda m, n: (0, 0)),
            pl.BlockSpec((tn, d), lambda m, n: (n, 0)),
            pl.BlockSpec((d, tn), lambda m, n: (0, n + v_first)),
            pl.BlockSpec((d, LANES), lambda m, n: (0, 0)),
            pl.BlockSpec((head_dim, 1), lambda m, n: (0, 0)),
        ],
        out_specs=[
            pl.BlockSpec((tn, tm), lambda m, n: (n, m)),
            pl.BlockSpec((tm, tn), lambda m, n: (m, n)),
            pl.BlockSpec((tm, LANES), lambda m, n: (m, 0)),
        ],
        out_shape=[
            jax.ShapeDtypeStruct((width, t), BF16),
            jax.ShapeDtypeStruct((t, width), BF16),
            jax.ShapeDtypeStruct((t, LANES), F32),
        ],
        scratch_shapes=[pltpu.VMEM((tm, d), BF16)],
        compiler_params=_params(("parallel", "arbitrary"), vmem + (8 << 20)),
        name="kv_proj",
    )(h, gain.reshape(1, d), w_kt, w_kvf, w_f, k_gain.reshape(head_dim, 1))


def _forget_cumsum_kernel(f_ref, b_ref, c_ref):
    rows, seq = f_ref.shape
    lane = lax.broadcasted_iota(jnp.int32, (rows, LANES), 1)
    carry = jnp.zeros((rows, 1), F32)
    for blk in range(seq // LANES):
        cols = slice(blk * LANES, (blk + 1) * LANES)
        x = _log_sigmoid(f_ref[:, cols] + b_ref[...])
        d = 1
        while d < LANES:
            x = x + jnp.where(lane >= d, pltpu.roll(x, d, axis=1), 0.0)
            d *= 2
        x = x + carry
        c_ref[:, cols] = x
        carry = x[:, LANES - 1:LANES]


def _forget_cumsum(f_rows, b_rows):
    rows, seq = f_rows.shape
    return pl.pallas_call(
        _forget_cumsum_kernel,
        out_shape=jax.ShapeDtypeStruct((rows, seq), F32),
        name="forget_cumsum",
    )(f_rows, b_rows)


def _qg_kernel(h_ref, g_ref, wq_ref, wg_ref, qn_ref, q_ref, sg_ref, xn_ref, *, head_dim):
    _store_xn(xn_ref, h_ref, g_ref)
    xn = xn_ref[...]
    _head_rmsnorm_store(q_ref, _wdot(xn, wq_ref), qn_ref[...], head_dim,
                        scale=head_dim ** -0.5 * LOG2E)
    sg_ref[...] = jax.nn.sigmoid(_wdot(xn, wg_ref)).astype(BF16)


def _qg_proj(h, gain, w_qg, q_gain, layer, tiles):
    t, d = h.shape
    width = w_qg.shape[2] // 2
    head_dim = q_gain.shape[0]
    tm, tn = min(tiles.proj_m, t), min(tiles.proj_n, width)
    nn = width // tn
    vmem = 2 * tm * d * 4 + tm * d * 2 + 2 * 2 * d * tn * 6 + 2 * 2 * tm * tn * 2 + 6 * tm * tn * 4
    return pl.pallas_call(
        functools.partial(_qg_kernel, head_dim=head_dim),
        grid=(t // tm, nn),
        in_specs=[
            pl.BlockSpec((tm, d), lambda m, n: (m, 0)),
            pl.BlockSpec((1, d), lambda m, n: (0, 0)),
            pl.BlockSpec((None, d, tn), lambda m, n: (layer, 0, n)),
            pl.BlockSpec((None, d, tn), lambda m, n: (layer, 0, n + nn)),
            pl.BlockSpec((1, head_dim), lambda m, n: (0, 0)),
        ],
        out_specs=[pl.BlockSpec((tm, tn), lambda m, n: (m, n))] * 2,
        out_shape=[jax.ShapeDtypeStruct((t, width), BF16)] * 2,
        scratch_shapes=[pltpu.VMEM((tm, d), BF16)],
        compiler_params=_params(("parallel", "arbitrary"), vmem + (8 << 20)),
        name="qg_proj",
    )(h, gain.reshape(1, d), w_qg, w_qg, q_gain.reshape(1, head_dim))


C_TERMS = 3


def _split_bf16_terms(x):
    terms = []
    for _ in range(C_TERMS):
        t = x.astype(BF16).astype(F32)
        terms.append(t)
        x = x - t
    return terms


def _attn_kernel(q_ref, kt_ref, v_ref, crow_ref, sg_ref, o_ref,
                 qa_ref, kta_ref, va_ref, m_ref, acc_ref, *, heads, head_dim):
    blk = q_ref.shape[0]
    nk = kta_ref.shape[1]
    hd = head_dim
    qi = pl.program_id(2)

    @pl.when(qi == 0)
    def _():
        sub = lax.broadcasted_iota(jnp.int32, (2 * SUBLANES, blk), 0)
        for h in range(heads):
            for kb in range(nk):
                cols = slice(kb * blk, (kb + 1) * blk)
                c_s = _split_bf16_terms(crow_ref[h, kb] * LOG2E)
                tail = jnp.where(sub < C_TERMS, 1.0, 0.0)
                for j, term in enumerate(c_s):
                    tail = jnp.where(sub == C_TERMS + j, -term, tail)
                kta_ref[h, kb, :hd, :] = kt_ref[h * hd:(h + 1) * hd, cols]
                kta_ref[h, kb, hd:hd + 2 * SUBLANES, :] = tail.astype(BF16)
                kta_ref[h, kb, hd + 2 * SUBLANES:, :] = jnp.zeros((hd - 2 * SUBLANES, blk), BF16)
            va_ref[h, :, :hd] = v_ref[:, h * hd:(h + 1) * hd]
            va_ref[h, :, hd:] = jnp.ones((v_ref.shape[0], hd), BF16)

    sub = lax.broadcasted_iota(jnp.int32, (hd, hd), 0)
    for h in range(heads):
        c_t = _split_bf16_terms(crow_ref[h, qi] * LOG2E)
        qa_ref[h, :, :hd] = q_ref[:, h * hd:(h + 1) * hd]
        for rb in range(blk // hd):
            rows = slice(rb * hd, (rb + 1) * hd)
            tail_t = jnp.where(jnp.logical_and(sub >= C_TERMS, sub < 2 * C_TERMS), 1.0, 0.0)
            for j, term in enumerate(c_t):
                tail_t = jnp.where(sub == j, term[:, rows], tail_t)
            qa_ref[h, rows, hd:] = tail_t.T.astype(BF16)

    def blocks(k0, count, diagonal):
        start = pl.multiple_of(k0 * blk, blk)
        scores = []
        for h in range(heads):
            chunks = []
            for b in range(count):
                s = _dot(qa_ref[h], kta_ref[h, k0 + b])
                if diagonal:
                    row = lax.broadcasted_iota(jnp.int32, (blk, blk), 0)
                    col = lax.broadcasted_iota(jnp.int32, (blk, blk), 1)
                    s = jnp.where(row >= col, s, -jnp.inf)
                chunks += [s[:, c * LANES:(c + 1) * LANES] for c in range(blk // LANES)]
            scores.append(chunks)
        probs = []
        for h, chunks in enumerate(scores):
            m_new = jnp.max(functools.reduce(jnp.maximum, chunks), axis=1, keepdims=True)
            if diagonal:
                m_new = jnp.broadcast_to(m_new, (blk, LANES))
                alpha = None
            else:
                m_old = m_ref[h]
                m_new = jnp.maximum(m_old, m_new)
                alpha = jnp.exp2(m_old - m_new)
            m_ref[h] = m_new
            p = jnp.concatenate([jnp.exp2(c - m_new).astype(BF16) for c in chunks], axis=1)
            probs.append((p, alpha))
        for h, (p, alpha) in enumerate(probs):
            pv = _dot(p, va_ref[h, pl.ds(start, count * blk), :])
            if diagonal:
                acc_ref[h] = pv
            else:
                acc_ref[h] = jnp.concatenate([alpha] * (2 * hd // LANES), axis=1) * acc_ref[h] + pv

    blocks(qi, 1, True)
    odd = qi % 2

    @pl.when(odd == 1)
    def _():
        blocks(0, 1, False)

    def pair(i, carry):
        blocks(2 * i + odd, 2, False)
        return carry

    lax.fori_loop(0, qi // 2, pair, 0)
    for h in range(heads):
        acc = acc_ref[h]
        o = acc[:, :hd] / acc[:, hd:]
        cols = slice(h * hd, (h + 1) * hd)
        o_ref[:, cols] = (o * sg_ref[:, cols].astype(F32)).astype(o_ref.dtype)


def _attention(q, kt, v, c, sg, batch, heads, tiles):
    t, width = q.shape
    seq = t // batch
    hd = width // heads
    assert hd == LANES
    blk = min(tiles.attn, seq)
    nq = seq // blk
    hp = min(tiles.attn_heads, heads)
    c_row = c.reshape(batch * heads, nq, 1, blk)
    qmap = lambda b, g, i: (b * nq + i, g)
    vmem = (2 * 2 * seq * hp * hd * 2 + 2 * seq * hp * 2 * hd * 2 + 2 * 3 * blk * hp * hd * 2
            + hp * blk * (2 * hd * 2 + LANES * 4 + 2 * hd * 4) + 2 * hp * seq * 4 * SUBLANES
            + 2 * hp * blk * LANES * 4 + hp * 3 * blk * blk * 4)
    return pl.pallas_call(
        functools.partial(_attn_kernel, heads=hp, head_dim=hd),
        grid=(batch, heads // hp, nq),
        in_specs=[
            pl.BlockSpec((blk, hp * hd), qmap),
            pl.BlockSpec((hp * hd, seq), lambda b, g, i: (g, b)),
            pl.BlockSpec((seq, hp * hd), lambda b, g, i: (b, g)),
            pl.BlockSpec((hp, nq, 1, blk), lambda b, g, i: (b * (heads // hp) + g, 0, 0, 0)),
            pl.BlockSpec((blk, hp * hd), qmap),
        ],
        out_specs=pl.BlockSpec((blk, hp * hd), qmap),
        out_shape=jax.ShapeDtypeStruct((t, width), BF16),
        scratch_shapes=[
            pltpu.VMEM((hp, blk, 2 * hd), BF16),
            pltpu.VMEM((hp, nq, 2 * hd, blk), BF16),
            pltpu.VMEM((hp, seq, 2 * hd), BF16),
            pltpu.VMEM((hp, blk, LANES), F32),
            pltpu.VMEM((hp, blk, 2 * hd), F32),
        ],
        compiler_params=_params(("parallel", "parallel", "arbitrary"), vmem + (8 << 20)),
        name="fox_attention",
    )(q, kt, v, c_row, sg)


def _out_proj_kernel(h_ref, a_ref, w_ref, o_ref):
    o_ref[...] = h_ref[...] + _dot(a_ref[...], w_ref[...])


def _out_proj(h, a, w, layer, tiles):
    t, d = h.shape
    kdim = a.shape[1]
    tm = min(tiles.gate_m, t)
    vmem = 4 * tm * d * 4 + 2 * tm * kdim * 2 + kdim * d * 2 + 2 * tm * d * 4
    return pl.pallas_call(
        _out_proj_kernel,
        grid=(t // tm,),
        in_specs=[
            pl.BlockSpec((tm, d), lambda m: (m, 0)),
            pl.BlockSpec((tm, kdim), lambda m: (m, 0)),
            pl.BlockSpec((None, kdim, d), lambda m: (layer, 0, 0), pipeline_mode=pl.Buffered(1)),
        ],
        out_specs=pl.BlockSpec((tm, d), lambda m: (m, 0)),
        out_shape=jax.ShapeDtypeStruct((t, d), F32),
        compiler_params=_params(("parallel",), vmem + (8 << 20)),
        name="b_out_proj",
    )(h, a, w)


def _forward(x, a_norm, a_w_in, a_v_norm, a_w_s, a_b_s, a_w_out, kv_norm, w_kvf, b_f, k_norm,
             b_norm, b_w_qg, q_norm, b_w_out, f_norm, f_w_up, f_conv_w, f_conv_b, f_w_down,
             final_norm, tiles=Tiles()):
    batch, seq, d = x.shape
    depth = f_norm.shape[0]
    n_a = a_norm.shape[0]
    heads = b_f.shape[0]
    att = heads * k_norm.shape[0]
    h = x.reshape(batch * seq, d)
    bf = lambda w: w.astype(BF16)
    a_w_s, a_w_out, b_w_out = bf(a_w_s), bf(a_w_out), bf(b_w_out)
    ffn_dim = f_w_down.shape[1]
    cast_in_ffn = _can_cast_in_ffn(batch * seq, d, seq, ffn_dim, tiles)
    w_up, w_down = bf(f_w_up[0]), bf(f_w_down[0])
    k = v = c = None
    for l in range(depth):
        if l < n_a:
            u, vv = _a_in(h, a_norm[l], a_w_in, l, tiles)
            h = _a_gate(h, u, vv, a_v_norm[l], a_w_s, a_b_s[l], a_w_out, l, tiles)
        else:
            j = l - n_a
            q, sg = _qg_proj(h, b_norm[j], b_w_qg, q_norm[j], j, tiles)
            og = _attention(q, k, v, c, sg, batch, heads, tiles)
            h = _out_proj(h, og, b_w_out, j, tiles)
        if l == depth - 1:
            h = _ffn(h, f_norm[l], w_up, f_conv_w, f_conv_b, w_down, l, seq, tiles,
                     final_gain=final_norm)
        elif cast_in_ffn:
            h, w_up, w_down = _ffn(h, f_norm[l], w_up, f_conv_w, f_conv_b, w_down, l, seq, tiles,
                                   next_weights=(f_w_up, f_w_down, l + 1))
        else:
            h = _ffn(h, f_norm[l], w_up, f_conv_w, f_conv_b, w_down, l, seq, tiles)
            w_up, w_down = bf(f_w_up[l + 1]), bf(f_w_down[l + 1])
        if l == n_a - 1:
            w_f = jnp.pad(w_kvf[:, 2 * att:], ((0, 0), (0, LANES - heads)))
            k, v, f = _kv_proj(h, kv_norm, w_kvf[:, :att].T, w_kvf, w_f, k_norm, tiles)
            f_rows = f[:, :heads].reshape(batch, seq, heads).transpose(0, 2, 1)
            c = _forget_cumsum(f_rows.reshape(batch * heads, seq),
                               jnp.tile(b_f, batch).reshape(batch * heads, 1))
    return h.reshape(batch, seq, d)


def kernel(x, a_norm, a_w_in, a_v_norm, a_w_s, a_b_s, a_w_out, kv_norm, w_kvf, b_f, k_norm,
           b_norm, b_w_qg, q_norm, b_w_out, f_norm, f_w_up, f_conv_w, f_conv_b, f_w_down,
           final_norm):
    return _forward(x, a_norm, a_w_in, a_v_norm, a_w_s, a_b_s, a_w_out, kv_norm, w_kvf, b_f,
                    k_norm, b_norm, b_w_qg, q_norm, b_w_out, f_norm, f_w_up, f_conv_w, f_conv_b,
                    f_w_down, final_norm)
```

```python
import functools
from typing import NamedTuple

import jax
import jax.numpy as jnp
from jax import lax
from jax.experimental import pallas as pl
from jax.experimental.pallas import tpu as pltpu

F32 = jnp.float32
BF16 = jnp.bfloat16
EPS = 1e-6
LOG2E = 1.4426950408889634
SUBLANES = 8
LANES = 128
MXU_COLS = 256
V7X_VMEM_BYTES = 64 * 1024 * 1024


class Tiles(NamedTuple):
    proj_m: int = 1024
    proj_n: int = 1024
    gate_m: int = 512
    ffn_m: int = 1024
    ffn_n: int = 512
    attn: int = 512
    attn_heads: int = 4


def _params(semantics, vmem_bytes):
    return pltpu.CompilerParams(dimension_semantics=semantics,
                                vmem_limit_bytes=min(vmem_bytes, V7X_VMEM_BYTES - (4 << 20)))


def _rmsnorm(x, gain):
    r = lax.rsqrt(jnp.mean(x * x, axis=-1, keepdims=True) + EPS)
    return (x * r) * gain


def _head_rmsnorm_store(o_ref, x, gain, head_dim, scale=None):
    for i in range(x.shape[1] // head_dim):
        xh = x[:, i * head_dim:(i + 1) * head_dim]
        y = _rmsnorm(xh, gain)
        if scale is not None:
            y = y * scale
        o_ref[:, i * head_dim:(i + 1) * head_dim] = y.astype(o_ref.dtype)


def _gelu(x):
    return 0.5 * x * (1.0 + lax.erf(x * (0.5 ** 0.5)))


def _log_sigmoid(x):
    return jnp.minimum(x, 0.0) - jnp.log1p(jnp.exp(-jnp.abs(x)))


def _dot(a, b):
    return jnp.dot(a, b, preferred_element_type=F32)


def _store_xn(xn_ref, h_ref, g_ref):
    @pl.when(pl.program_id(1) == 0)
    def _():
        xn_ref[...] = _rmsnorm(h_ref[...], g_ref[...]).astype(BF16)


def _a_in_kernel(h_ref, g_ref, wu_ref, wv_ref, u_ref, v_ref, xn_ref):
    _store_xn(xn_ref, h_ref, g_ref)
    xn = xn_ref[...]
    u_ref[...] = _gelu(_dot(xn, wu_ref[...])).astype(BF16)
    v_ref[...] = _gelu(_dot(xn, wv_ref[...])).astype(BF16)


def _a_in(h, gain, w_in, layer, tiles):
    t, d = h.shape
    width = w_in.shape[2] // 2
    tm, tn = min(tiles.proj_m, t), min(tiles.proj_n, width)
    nn = width // tn
    vmem = 2 * tm * d * 4 + tm * d * 2 + 2 * 2 * d * tn * 2 + 2 * 2 * tm * tn * 2 + 6 * tm * tn * 4
    return pl.pallas_call(
        _a_in_kernel,
        grid=(t // tm, nn),
        in_specs=[
            pl.BlockSpec((tm, d), lambda m, n: (m, 0)),
            pl.BlockSpec((1, d), lambda m, n: (0, 0)),
            pl.BlockSpec((None, d, tn), lambda m, n: (layer, 0, n)),
            pl.BlockSpec((None, d, tn), lambda m, n: (layer, 0, n + nn)),
        ],
        out_specs=[pl.BlockSpec((tm, tn), lambda m, n: (m, n))] * 2,
        out_shape=[jax.ShapeDtypeStruct((t, width), BF16)] * 2,
        scratch_shapes=[pltpu.VMEM((tm, d), BF16)],
        compiler_params=_params(("parallel", "arbitrary"), vmem + (8 << 20)),
        name="a_in_proj",
    )(h, gain.reshape(1, d), w_in, w_in)


def _a_gate_kernel(h_ref, u_ref, v_ref, gn_ref, ws_ref, bt_ref, wo_ref, o_ref, gated_ref,
                   *, chunk, groups):
    tm, width = v_ref.shape
    gdim = width // groups
    row = lax.broadcasted_iota(jnp.int32, (chunk, chunk), 0)
    col = lax.broadcasted_iota(jnp.int32, (chunk, chunk), 1)
    causal = row >= col
    parts = 2 if (tm // chunk) % 2 == 0 else 1
    nchunks = tm // chunk // parts
    for part in range(parts):
        base = part * nchunks * chunk
        part_rows = slice(base, base + nchunks * chunk)
        vs = _rmsnorm(v_ref[part_rows, :].astype(F32), gn_ref[...]).astype(BF16)
        for g in range(groups):
            w = jnp.where(causal, ws_ref[g], jnp.zeros((), BF16))
            bias = bt_ref[:, g:g + 1]
            cols = slice(g * gdim, (g + 1) * gdim)
            rhs = jnp.concatenate([vs[c * chunk:(c + 1) * chunk, cols] for c in range(nchunks)],
                                  axis=1)
            mixed = _dot(w, rhs) + bias
            for c in range(nchunks):
                rows = slice(base + c * chunk, base + (c + 1) * chunk)
                gated_ref[rows, cols] = (u_ref[rows, cols].astype(F32)
                                         * mixed[:, c * gdim:(c + 1) * gdim]).astype(BF16)
        o_ref[part_rows, :] = h_ref[part_rows, :] + _dot(gated_ref[part_rows, :], wo_ref[...])


def _a_gate(h, u, v, v_gain, w_s, b_s, w_out, layer, tiles):
    t, d = h.shape
    width = u.shape[1]
    _, groups, chunk, _ = w_s.shape
    tm = min(tiles.gate_m, t)
    vmem = (4 * tm * d * 4 + 4 * tm * width * 2 + 2 * width * d * 2 + tm * width * 2
            + 3 * tm * width * 4 + tm * d * 4)
    return pl.pallas_call(
        functools.partial(_a_gate_kernel, chunk=chunk, groups=groups),
        grid=(t // tm,),
        in_specs=[
            pl.BlockSpec((tm, d), lambda m: (m, 0)),
            pl.BlockSpec((tm, width), lambda m: (m, 0)),
            pl.BlockSpec((tm, width), lambda m: (m, 0)),
            pl.BlockSpec((1, width), lambda m: (0, 0)),
            pl.BlockSpec((None, groups, chunk, chunk), lambda m: (layer, 0, 0, 0)),
            pl.BlockSpec((chunk, groups), lambda m: (0, 0)),
            pl.BlockSpec((None, width, d), lambda m: (layer, 0, 0), pipeline_mode=pl.Buffered(1)),
        ],
        out_specs=pl.BlockSpec((tm, d), lambda m: (m, 0)),
        out_shape=jax.ShapeDtypeStruct((t, d), F32),
        scratch_shapes=[pltpu.VMEM((tm, width), BF16)],
        compiler_params=_params(("parallel",), vmem + (8 << 20)),
        name="a_gate_out_proj",
    )(h, u, v, v_gain.reshape(1, width), w_s, b_s.T, w_out)


def _ffn_kernel(*refs, tiles_per_seq, nj, final, cast_next):
    (h_ref, g_ref, wg_ref, wv_ref, cwg_ref, cwv_ref, cbg_ref, cbv_ref, wd_ref), refs = refs[:9], refs[9:]
    if final:
        fg_ref, refs = refs[0], refs[1:]
    if cast_next:
        (next_up_ref, next_down_ref), refs = refs[:2], refs[2:]
    o_ref, refs = refs[0], refs[1:]
    if cast_next:
        (next_up_out, next_down_out), refs = refs[:2], refs[2:]
    xn_ref, acc_ref, carry_ref, act0_ref, act1_ref, upg_ref, upv_ref = refs

    def cast_side_stream():
        if cast_next:
            next_up_out[...] = next_up_ref[...].astype(BF16)
            next_down_out[...] = next_down_ref[...].astype(BF16)
    acts = (act0_ref, act1_ref)
    m, j = pl.program_id(0), pl.program_id(1)
    tm, d = h_ref.shape
    tn = wg_ref.shape[1]
    seq_start = (m % tiles_per_seq) == 0

    @pl.when(j == 0)
    def _():
        xn_ref[...] = _rmsnorm(h_ref[...], g_ref[...]).astype(BF16)
        acc_ref[...] = jnp.zeros(acc_ref.shape, F32)

    @pl.when(jnp.logical_and(seq_start, j < nj))
    def _():
        carry_ref[2 * j] = jnp.zeros(carry_ref.shape[1:], F32)
        carry_ref[2 * j + 1] = jnp.zeros(carry_ref.shape[1:], F32)

    def up_dots():
        return _dot(xn_ref[...], wg_ref[...]), _dot(xn_ref[...], wv_ref[...])

    def conv(up, cw_ref, cb_ref, slot, up_ref):
        up_ref[:SUBLANES] = carry_ref[slot]
        up_ref[SUBLANES:] = up
        carry_ref[slot] = up[tm - SUBLANES:, :]
        w0, w1, w2, b = cw_ref[0:1], cw_ref[1:2], cw_ref[2:3], cb_ref[...]
        return (up * w2 + up_ref[SUBLANES - 1:SUBLANES - 1 + tm] * w1
                + up_ref[SUBLANES - 2:SUBLANES - 2 + tm] * w0 + b)

    def gate_act(ups, act_ref):
        gate = conv(ups[0], cwg_ref, cbg_ref, 2 * j, upg_ref)
        val = conv(ups[1], cwv_ref, cbv_ref, 2 * j + 1, upv_ref)
        act_ref[...] = (jax.nn.silu(gate) * val).astype(BF16)

    def down_proj(act_ref):
        acc_ref[...] += _dot(act_ref[...], wd_ref[...])

    @pl.when(j == 0)
    def _():
        cast_side_stream()
        gate_act(up_dots(), acts[0])

    for parity in range(2):
        @pl.when(jnp.logical_and(jnp.logical_and(j > 0, j < nj), j % 2 == parity))
        def _():
            cast_side_stream()
            ups = up_dots()
            down_proj(acts[1 - parity])
            gate_act(ups, acts[parity])

    @pl.when(j == nj)
    def _():
        cast_side_stream()
        down_proj(acts[(nj - 1) % 2])
        y = h_ref[...] + acc_ref[...]
        if final:
            y = _rmsnorm(y, fg_ref[...])
        o_ref[...] = y


def _ffn_tiles(t, seq_len, ffn, tiles):
    tm, tn = min(tiles.ffn_m, seq_len), min(tiles.ffn_n, ffn)
    assert seq_len % tm == 0 and ffn % tn == 0
    return tm, tn, t // tm, ffn // tn


def _can_cast_in_ffn(t, d, seq_len, ffn, tiles):
    _, _, nm, nj = _ffn_tiles(t, seq_len, ffn, tiles)
    return (d % nm == 0 and (d // nm) % LANES == 0 and (2 * ffn) % nj == 0
            and (2 * ffn // nj) % LANES == 0 and (ffn // nj) % (2 * SUBLANES) == 0)


def _ffn(h, gain, w_up, conv_w, conv_b, w_down, layer, seq_len, tiles, final_gain=None,
         next_weights=None):
    t, d = h.shape
    ffn = w_down.shape[0]
    tm, tn, nm, nj = _ffn_tiles(t, seq_len, ffn, tiles)
    final = final_gain is not None
    cast_next = next_weights is not None
    row = lambda m, j: (m, 0)
    const = lambda m, j: (0, 0)
    up_step = lambda j: jnp.minimum(j, nj - 1)
    in_specs = [
        pl.BlockSpec((tm, d), row, pipeline_mode=pl.Buffered(1)),
        pl.BlockSpec((1, d), const),
        pl.BlockSpec((d, tn), lambda m, j: (0, up_step(j))),
        pl.BlockSpec((d, tn), lambda m, j: (0, up_step(j) + nj)),
        pl.BlockSpec((None, conv_w.shape[1], tn), lambda m, j: (layer, 0, up_step(j))),
        pl.BlockSpec((None, conv_w.shape[1], tn), lambda m, j: (layer, 0, up_step(j) + nj)),
        pl.BlockSpec((None, 1, tn), lambda m, j: (layer, 0, up_step(j))),
        pl.BlockSpec((None, 1, tn), lambda m, j: (layer, 0, up_step(j) + nj)),
        pl.BlockSpec((tn, d), lambda m, j: (jnp.maximum(j - 1, 0), 0)),
    ]
    conv_b = conv_b.reshape(conv_b.shape[0], 1, -1)
    args = [h, gain.reshape(1, d), w_up, w_up, conv_w, conv_w, conv_b, conv_b, w_down]
    out_specs = [pl.BlockSpec((tm, d), row, pipeline_mode=pl.Buffered(1))]
    out_shape = [jax.ShapeDtypeStruct((t, d), F32)]
    if final:
        in_specs.append(pl.BlockSpec((1, d), const))
        args.append(final_gain.reshape(1, d))
    if cast_next:
        nxt_up, nxt_down, nxt = next_weights
        up_blk, down_blk = (d // nm, 2 * ffn // nj), (ffn // nj, d // nm)
        in_specs += [pl.BlockSpec((None,) + up_blk, lambda m, j: (nxt, m, up_step(j))),
                     pl.BlockSpec((None,) + down_blk, lambda m, j: (nxt, up_step(j), m))]
        out_specs += [pl.BlockSpec(up_blk, lambda m, j: (m, up_step(j))),
                      pl.BlockSpec(down_blk, lambda m, j: (up_step(j), m))]
        out_shape += [jax.ShapeDtypeStruct((d, 2 * ffn), BF16),
                      jax.ShapeDtypeStruct((ffn, d), BF16)]
        args += [nxt_up, nxt_down]
    vmem = (2 * tm * d * 4 + tm * d * 2 + tm * d * 4 + 2 * 3 * d * tn * 2 + 2 * tm * tn * 2
            + 8 * tm * tn * 4 + (2 * 6 * 3 * d * ffn // (nm * nj) if cast_next else 0))
    outs = pl.pallas_call(
        functools.partial(_ffn_kernel, tiles_per_seq=seq_len // tm, nj=nj, final=final,
                          cast_next=cast_next),
        grid=(nm, nj + 1),
        in_specs=in_specs,
        out_specs=out_specs,
        out_shape=out_shape,
        scratch_shapes=[
            pltpu.VMEM((tm, d), BF16),
            pltpu.VMEM((tm, d), F32),
            pltpu.VMEM((2 * nj, SUBLANES, tn), F32),
            pltpu.VMEM((tm, tn), BF16),
            pltpu.VMEM((tm, tn), BF16),
            pltpu.VMEM((SUBLANES + tm, tn), F32),
            pltpu.VMEM((SUBLANES + tm, tn), F32),
        ],
        compiler_params=_params(("arbitrary", "arbitrary"), vmem + (8 << 20)),
        name="conv_ffn",
    )(*args)
    return outs if cast_next else outs[0]


def _kv_kernel(h_ref, g_ref, wkt_ref, wv_ref, wf_ref, kn_ref, kt_ref, v_ref, f_ref, xn_ref,
               *, head_dim):
    _store_xn(xn_ref, h_ref, g_ref)
    xn = xn_ref[...]
    kt = lax.dot_general(wkt_ref[...], xn, (((1,), (1,)), ((), ())), preferred_element_type=F32)
    for i in range(kt.shape[0] // head_dim):
        rows = slice(i * head_dim, (i + 1) * head_dim)
        x = kt[rows]
        r = lax.rsqrt(jnp.mean(x * x, axis=0, keepdims=True) + EPS)
        kt_ref[rows, :] = ((x * r) * kn_ref[...]).astype(BF16)
    v_ref[...] = _dot(xn, wv_ref[...]).astype(BF16)

    @pl.when(pl.program_id(1) == 0)
    def _():
        f_ref[...] = _dot(xn, wf_ref[...])


def _kv_proj(h, gain, w_kt, w_kvf, w_f, k_gain, tiles):
    t, d = h.shape
    width = w_kt.shape[0]
    head_dim = k_gain.shape[0]
    tm, tn = min(tiles.proj_m, t), min(tiles.proj_n, width)
    v_first = width // tn
    vmem = (2 * tm * d * 4 + tm * d * 2 + 2 * 2 * d * tn * 2 + 2 * d * LANES * 2
            + 2 * 2 * tm * tn * 2 + 2 * tm * LANES * 4 + 6 * tm * tn * 4)
    return pl.pallas_call(
        functools.partial(_kv_kernel, head_dim=head_dim),
        grid=(t // tm, width // tn),
        in_specs=[
            pl.BlockSpec((tm, d), lambda m, n: (m, 0)),
            pl.BlockSpec((1, d), lambda m, n: (0, 0)),
            pl.BlockSpec((tn, d), lambda m, n: (n, 0)),
            pl.BlockSpec((d, tn), lambda m, n: (0, n + v_first)),
            pl.BlockSpec((d, LANES), lambda m, n: (0, 0)),
            pl.BlockSpec((head_dim, 1), lambda m, n: (0, 0)),
        ],
        out_specs=[
            pl.BlockSpec((tn, tm), lambda m, n: (n, m)),
            pl.BlockSpec((tm, tn), lambda m, n: (m, n)),
            pl.BlockSpec((tm, LANES), lambda m, n: (m, 0)),
        ],
        out_shape=[
            jax.ShapeDtypeStruct((width, t), BF16),
            jax.ShapeDtypeStruct((t, width), BF16),
            jax.ShapeDtypeStruct((t, LANES), F32),
        ],
        scratch_shapes=[pltpu.VMEM((tm, d), BF16)],
        compiler_params=_params(("parallel", "arbitrary"), vmem + (8 << 20)),
        name="kv_proj",
    )(h, gain.reshape(1, d), w_kt, w_kvf, w_f, k_gain.reshape(head_dim, 1))


def _forget_cumsum_kernel(f_ref, b_ref, c_ref):
    rows, seq = f_ref.shape
    lane = lax.broadcasted_iota(jnp.int32, (rows, LANES), 1)
    carry = jnp.zeros((rows, 1), F32)
    for blk in range(seq // LANES):
        cols = slice(blk * LANES, (blk + 1) * LANES)
        x = _log_sigmoid(f_ref[:, cols] + b_ref[...])
        d = 1
        while d < LANES:
            x = x + jnp.where(lane >= d, pltpu.roll(x, d, axis=1), 0.0)
            d *= 2
        x = x + carry
        c_ref[:, cols] = x
        carry = x[:, LANES - 1:LANES]


def _forget_cumsum(f_rows, b_rows):
    rows, seq = f_rows.shape
    return pl.pallas_call(
        _forget_cumsum_kernel,
        out_shape=jax.ShapeDtypeStruct((rows, seq), F32),
        name="forget_cumsum",
    )(f_rows, b_rows)


def _qg_kernel(h_ref, g_ref, wq_ref, wg_ref, qn_ref, q_ref, sg_ref, xn_ref, *, head_dim):
    _store_xn(xn_ref, h_ref, g_ref)
    xn = xn_ref[...]
    _head_rmsnorm_store(q_ref, _dot(xn, wq_ref[...]), qn_ref[...], head_dim,
                        scale=head_dim ** -0.5 * LOG2E)
    sg_ref[...] = jax.nn.sigmoid(_dot(xn, wg_ref[...])).astype(BF16)


def _qg_proj(h, gain, w_qg, q_gain, layer, tiles):
    t, d = h.shape
    width = w_qg.shape[2] // 2
    head_dim = q_gain.shape[0]
    tm, tn = min(tiles.proj_m, t), min(tiles.proj_n, width)
    nn = width // tn
    vmem = 2 * tm * d * 4 + tm * d * 2 + 2 * 2 * d * tn * 2 + 2 * 2 * tm * tn * 2 + 6 * tm * tn * 4
    return pl.pallas_call(
        functools.partial(_qg_kernel, head_dim=head_dim),
        grid=(t // tm, nn),
        in_specs=[
            pl.BlockSpec((tm, d), lambda m, n: (m, 0)),
            pl.BlockSpec((1, d), lambda m, n: (0, 0)),
            pl.BlockSpec((None, d, tn), lambda m, n: (layer, 0, n)),
            pl.BlockSpec((None, d, tn), lambda m, n: (layer, 0, n + nn)),
            pl.BlockSpec((1, head_dim), lambda m, n: (0, 0)),
        ],
        out_specs=[pl.BlockSpec((tm, tn), lambda m, n: (m, n))] * 2,
        out_shape=[jax.ShapeDtypeStruct((t, width), BF16)] * 2,
        scratch_shapes=[pltpu.VMEM((tm, d), BF16)],
        compiler_params=_params(("parallel", "arbitrary"), vmem + (8 << 20)),
        name="qg_proj",
    )(h, gain.reshape(1, d), w_qg, w_qg, q_gain.reshape(1, head_dim))


C_TERMS = 3


def _split_bf16_terms(x):
    terms = []
    for _ in range(C_TERMS):
        t = x.astype(BF16).astype(F32)
        terms.append(t)
        x = x - t
    return terms


def _attn_kernel(q_ref, kt_ref, v_ref, crow_ref, sg_ref, o_ref,
                 qa_ref, kta_ref, va_ref, m_ref, acc_ref, *, heads, head_dim):
    blk = q_ref.shape[0]
    nk = kta_ref.shape[1]
    hd = head_dim
    qi = pl.program_id(2)

    @pl.when(qi == 0)
    def _():
        sub = lax.broadcasted_iota(jnp.int32, (2 * SUBLANES, blk), 0)
        for h in range(heads):
            for kb in range(nk):
                cols = slice(kb * blk, (kb + 1) * blk)
                c_s = _split_bf16_terms(crow_ref[h, kb] * LOG2E)
                tail = jnp.where(sub < C_TERMS, 1.0, 0.0)
                for j, term in enumerate(c_s):
                    tail = jnp.where(sub == C_TERMS + j, -term, tail)
                kta_ref[h, kb, :hd, :] = kt_ref[h * hd:(h + 1) * hd, cols]
                kta_ref[h, kb, hd:hd + 2 * SUBLANES, :] = tail.astype(BF16)
                kta_ref[h, kb, hd + 2 * SUBLANES:, :] = jnp.zeros((hd - 2 * SUBLANES, blk), BF16)
            va_ref[h, :, :hd] = v_ref[:, h * hd:(h + 1) * hd]
            va_ref[h, :, hd:] = jnp.ones((v_ref.shape[0], hd), BF16)

    sub = lax.broadcasted_iota(jnp.int32, (hd, hd), 0)
    for h in range(heads):
        c_t = _split_bf16_terms(crow_ref[h, qi] * LOG2E)
        qa_ref[h, :, :hd] = q_ref[:, h * hd:(h + 1) * hd]
        for rb in range(blk // hd):
            rows = slice(rb * hd, (rb + 1) * hd)
            tail_t = jnp.where(jnp.logical_and(sub >= C_TERMS, sub < 2 * C_TERMS), 1.0, 0.0)
            for j, term in enumerate(c_t):
                tail_t = jnp.where(sub == j, term[:, rows], tail_t)
            qa_ref[h, rows, hd:] = tail_t.T.astype(BF16)

    def blocks(k0, count, diagonal):
        start = pl.multiple_of(k0 * blk, blk)
        scores = []
        for h in range(heads):
            chunks = []
            for b in range(count):
                s = _dot(qa_ref[h], kta_ref[h, k0 + b])
                if diagonal:
                    row = lax.broadcasted_iota(jnp.int32, (blk, blk), 0)
                    col = lax.broadcasted_iota(jnp.int32, (blk, blk), 1)
                    s = jnp.where(row >= col, s, -jnp.inf)
                chunks += [s[:, c * LANES:(c + 1) * LANES] for c in range(blk // LANES)]
            scores.append(chunks)
        probs = []
        for h, chunks in enumerate(scores):
            m_new = jnp.max(functools.reduce(jnp.maximum, chunks), axis=1, keepdims=True)
            if diagonal:
                m_new = jnp.broadcast_to(m_new, (blk, LANES))
                alpha = None
            else:
                m_old = m_ref[h]
                m_new = jnp.maximum(m_old, m_new)
                alpha = jnp.exp2(m_old - m_new)
            m_ref[h] = m_new
            p = jnp.concatenate([jnp.exp2(c - m_new).astype(BF16) for c in chunks], axis=1)
            probs.append((p, alpha))
        for h, (p, alpha) in enumerate(probs):
            pv = _dot(p, va_ref[h, pl.ds(start, count * blk), :])
            if diagonal:
                acc_ref[h] = pv
            else:
                acc_ref[h] = jnp.concatenate([alpha] * (2 * hd // LANES), axis=1) * acc_ref[h] + pv

    blocks(qi, 1, True)
    odd = qi % 2

    @pl.when(odd == 1)
    def _():
        blocks(0, 1, False)

    def pair(i, carry):
        blocks(2 * i + odd, 2, False)
        return carry

    lax.fori_loop(0, qi // 2, pair, 0)
    for h in range(heads):
        acc = acc_ref[h]
        o = acc[:, :hd] / acc[:, hd:]
        cols = slice(h * hd, (h + 1) * hd)
        o_ref[:, cols] = (o * sg_ref[:, cols].astype(F32)).astype(o_ref.dtype)


def _attention(q, kt, v, c, sg, batch, heads, tiles):
    t, width = q.shape
    seq = t // batch
    hd = width // heads
    assert hd == LANES
    blk = min(tiles.attn, seq)
    nq = seq // blk
    hp = min(tiles.attn_heads, heads)
    c_row = c.reshape(batch * heads, nq, 1, blk)
    qmap = lambda b, g, i: (b * nq + i, g)
    vmem = (2 * 2 * seq * hp * hd * 2 + 2 * seq * hp * 2 * hd * 2 + 2 * 3 * blk * hp * hd * 2
            + hp * blk * (2 * hd * 2 + LANES * 4 + 2 * hd * 4) + 2 * hp * seq * 4 * SUBLANES
            + 2 * hp * blk * LANES * 4 + hp * 3 * blk * blk * 4)
    return pl.pallas_call(
        functools.partial(_attn_kernel, heads=hp, head_dim=hd),
        grid=(batch, heads // hp, nq),
        in_specs=[
            pl.BlockSpec((blk, hp * hd), qmap),
            pl.BlockSpec((hp * hd, seq), lambda b, g, i: (g, b)),
            pl.BlockSpec((seq, hp * hd), lambda b, g, i: (b, g)),
            pl.BlockSpec((hp, nq, 1, blk), lambda b, g, i: (b * (heads // hp) + g, 0, 0, 0)),
            pl.BlockSpec((blk, hp * hd), qmap),
        ],
        out_specs=pl.BlockSpec((blk, hp * hd), qmap),
        out_shape=jax.ShapeDtypeStruct((t, width), BF16),
        scratch_shapes=[
            pltpu.VMEM((hp, blk, 2 * hd), BF16),
            pltpu.VMEM((hp, nq, 2 * hd, blk), BF16),
            pltpu.VMEM((hp, seq, 2 * hd), BF16),
            pltpu.VMEM((hp, blk, LANES), F32),
            pltpu.VMEM((hp, blk, 2 * hd), F32),
        ],
        compiler_params=_params(("parallel", "parallel", "arbitrary"), vmem + (8 << 20)),
        name="fox_attention",
    )(q, kt, v, c_row, sg)


def _out_proj_kernel(h_ref, a_ref, w_ref, o_ref):
    o_ref[...] = h_ref[...] + _dot(a_ref[...], w_ref[...])


def _out_proj(h, a, w, layer, tiles):
    t, d = h.shape
    kdim = a.shape[1]
    tm = min(tiles.gate_m, t)
    vmem = 4 * tm * d * 4 + 2 * tm * kdim * 2 + kdim * d * 2 + 2 * tm * d * 4
    return pl.pallas_call(
        _out_proj_kernel,
        grid=(t // tm,),
        in_specs=[
            pl.BlockSpec((tm, d), lambda m: (m, 0)),
            pl.BlockSpec((tm, kdim), lambda m: (m, 0)),
            pl.BlockSpec((None, kdim, d), lambda m: (layer, 0, 0), pipeline_mode=pl.Buffered(1)),
        ],
        out_specs=pl.BlockSpec((tm, d), lambda m: (m, 0)),
        out_shape=jax.ShapeDtypeStruct((t, d), F32),
        compiler_params=_params(("parallel",), vmem + (8 << 20)),
        name="b_out_proj",
    )(h, a, w)


def _forward(x, a_norm, a_w_in, a_v_norm, a_w_s, a_b_s, a_w_out, kv_norm, w_kvf, b_f, k_norm,
             b_norm, b_w_qg, q_norm, b_w_out, f_norm, f_w_up, f_conv_w, f_conv_b, f_w_down,
             final_norm, tiles=Tiles()):
    batch, seq, d = x.shape
    depth = f_norm.shape[0]
    n_a = a_norm.shape[0]
    heads = b_f.shape[0]
    att = heads * k_norm.shape[0]
    h = x.reshape(batch * seq, d)
    bf = lambda w: w.astype(BF16)
    a_w_in, a_w_s, a_w_out = bf(a_w_in), bf(a_w_s), bf(a_w_out)
    b_w_qg, b_w_out, w_kvf = bf(b_w_qg), bf(b_w_out), bf(w_kvf)
    ffn_dim = f_w_down.shape[1]
    cast_in_ffn = _can_cast_in_ffn(batch * seq, d, seq, ffn_dim, tiles)
    w_up, w_down = bf(f_w_up[0]), bf(f_w_down[0])
    k = v = c = None
    for l in range(depth):
        if l < n_a:
            u, vv = _a_in(h, a_norm[l], a_w_in, l, tiles)
            h = _a_gate(h, u, vv, a_v_norm[l], a_w_s, a_b_s[l], a_w_out, l, tiles)
        else:
            j = l - n_a
            q, sg = _qg_proj(h, b_norm[j], b_w_qg, q_norm[j], j, tiles)
            og = _attention(q, k, v, c, sg, batch, heads, tiles)
            h = _out_proj(h, og, b_w_out, j, tiles)
        if l == depth - 1:
            h = _ffn(h, f_norm[l], w_up, f_conv_w, f_conv_b, w_down, l, seq, tiles,
                     final_gain=final_norm)
        elif cast_in_ffn:
            h, w_up, w_down = _ffn(h, f_norm[l], w_up, f_conv_w, f_conv_b, w_down, l, seq, tiles,
                                   next_weights=(f_w_up, f_w_down, l + 1))
        else:
            h = _ffn(h, f_norm[l], w_up, f_conv_w, f_conv_b, w_down, l, seq, tiles)
            w_up, w_down = bf(f_w_up[l + 1]), bf(f_w_down[l + 1])
        if l == n_a - 1:
            w_f = jnp.pad(w_kvf[:, 2 * att:], ((0, 0), (0, LANES - heads)))
            k, v, f = _kv_proj(h, kv_norm, w_kvf[:, :att].T, w_kvf, w_f, k_norm, tiles)
            f_rows = f[:, :heads].reshape(batch, seq, heads).transpose(0, 2, 1)
            c = _forget_cumsum(f_rows.reshape(batch * heads, seq),
                               jnp.tile(b_f, batch).reshape(batch * heads, 1))
    return h.reshape(batch, seq, d)


def kernel(x, a_norm, a_w_in, a_v_norm, a_w_s, a_b_s, a_w_out, kv_norm, w_kvf, b_f, k_norm,
           b_norm, b_w_qg, q_norm, b_w_out, f_norm, f_w_up, f_conv_w, f_conv_b, f_w_down,
           final_norm):
    return _forward(x, a_norm, a_w_in, a_v_norm, a_w_s, a_b_s, a_w_out, kv_norm, w_kvf, b_f,
                    k_norm, b_norm, b_w_qg, q_norm, b_w_out, f_norm, f_w_up, f_conv_w, f_conv_b,
                    f_w_down, final_norm)
```

```python
import functools
from typing import NamedTuple

import jax
import jax.numpy as jnp
from jax import lax
from jax.experimental import pallas as pl
from jax.experimental.pallas import tpu as pltpu

F32 = jnp.float32
BF16 = jnp.bfloat16
EPS = 1e-6
LOG2E = 1.4426950408889634
SUBLANES = 8
LANES = 128
MXU_COLS = 256
V7X_VMEM_BYTES = 64 * 1024 * 1024


class Tiles(NamedTuple):
    proj_m: int = 1024
    proj_n: int = 1024
    gate_m: int = 512
    ffn_m: int = 1024
    ffn_n: int = 512
    attn: int = 512
    attn_heads: int = 4


def _params(semantics, vmem_bytes):
    return pltpu.CompilerParams(dimension_semantics=semantics,
                                vmem_limit_bytes=min(vmem_bytes, V7X_VMEM_BYTES - (4 << 20)))


def _rmsnorm(x, gain):
    r = lax.rsqrt(jnp.mean(x * x, axis=-1, keepdims=True) + EPS)
    return (x * r) * gain


def _head_rmsnorm_store(o_ref, x, gain, head_dim, scale=None):
    for i in range(x.shape[1] // head_dim):
        xh = x[:, i * head_dim:(i + 1) * head_dim]
        y = _rmsnorm(xh, gain)
        if scale is not None:
            y = y * scale
        o_ref[:, i * head_dim:(i + 1) * head_dim] = y.astype(o_ref.dtype)


def _gelu(x):
    return 0.5 * x * (1.0 + lax.erf(x * (0.5 ** 0.5)))


def _log_sigmoid(x):
    return jnp.minimum(x, 0.0) - jnp.log1p(jnp.exp(-jnp.abs(x)))


def _dot(a, b):
    return jnp.dot(a, b, preferred_element_type=F32)


def _store_xn(xn_ref, h_ref, g_ref):
    @pl.when(pl.program_id(1) == 0)
    def _():
        xn_ref[...] = _rmsnorm(h_ref[...], g_ref[...]).astype(BF16)


def _a_in_kernel(h_ref, g_ref, wu_ref, wv_ref, u_ref, v_ref, xn_ref):
    _store_xn(xn_ref, h_ref, g_ref)
    xn = xn_ref[...]
    u_ref[...] = _gelu(_dot(xn, wu_ref[...])).astype(BF16)
    v_ref[...] = _gelu(_dot(xn, wv_ref[...])).astype(BF16)


def _a_in(h, gain, w_in, layer, tiles):
    t, d = h.shape
    width = w_in.shape[2] // 2
    tm, tn = min(tiles.proj_m, t), min(tiles.proj_n, width)
    nn = width // tn
    vmem = 2 * tm * d * 4 + tm * d * 2 + 2 * 2 * d * tn * 2 + 2 * 2 * tm * tn * 2 + 6 * tm * tn * 4
    return pl.pallas_call(
        _a_in_kernel,
        grid=(t // tm, nn),
        in_specs=[
            pl.BlockSpec((tm, d), lambda m, n: (m, 0)),
            pl.BlockSpec((1, d), lambda m, n: (0, 0)),
            pl.BlockSpec((None, d, tn), lambda m, n: (layer, 0, n)),
            pl.BlockSpec((None, d, tn), lambda m, n: (layer, 0, n + nn)),
        ],
        out_specs=[pl.BlockSpec((tm, tn), lambda m, n: (m, n))] * 2,
        out_shape=[jax.ShapeDtypeStruct((t, width), BF16)] * 2,
        scratch_shapes=[pltpu.VMEM((tm, d), BF16)],
        compiler_params=_params(("parallel", "arbitrary"), vmem + (8 << 20)),
        name="a_in_proj",
    )(h, gain.reshape(1, d), w_in, w_in)


def _a_gate_kernel(h_ref, u_ref, v_ref, gn_ref, ws_ref, bt_ref, wo_ref, o_ref, gated_ref,
                   *, chunk, groups):
    tm, width = v_ref.shape
    gdim = width // groups
    row = lax.broadcasted_iota(jnp.int32, (chunk, chunk), 0)
    col = lax.broadcasted_iota(jnp.int32, (chunk, chunk), 1)
    causal = row >= col
    parts = 2 if (tm // chunk) % 2 == 0 else 1
    nchunks = tm // chunk // parts
    for part in range(parts):
        base = part * nchunks * chunk
        part_rows = slice(base, base + nchunks * chunk)
        vs = _rmsnorm(v_ref[part_rows, :].astype(F32), gn_ref[...]).astype(BF16)
        for g in range(groups):
            w = jnp.where(causal, ws_ref[g], jnp.zeros((), BF16))
            bias = bt_ref[:, g:g + 1]
            cols = slice(g * gdim, (g + 1) * gdim)
            rhs = jnp.concatenate([vs[c * chunk:(c + 1) * chunk, cols] for c in range(nchunks)],
                                  axis=1)
            mixed = _dot(w, rhs) + bias
            for c in range(nchunks):
                rows = slice(base + c * chunk, base + (c + 1) * chunk)
                gated_ref[rows, cols] = (u_ref[rows, cols].astype(F32)
                                         * mixed[:, c * gdim:(c + 1) * gdim]).astype(BF16)
        o_ref[part_rows, :] = h_ref[part_rows, :] + _dot(gated_ref[part_rows, :], wo_ref[...])


def _a_gate(h, u, v, v_gain, w_s, b_s, w_out, layer, tiles):
    t, d = h.shape
    width = u.shape[1]
    _, groups, chunk, _ = w_s.shape
    tm = min(tiles.gate_m, t)
    vmem = (4 * tm * d * 4 + 4 * tm * width * 2 + 2 * width * d * 2 + tm * width * 2
            + 3 * tm * width * 4 + tm * d * 4)
    return pl.pallas_call(
        functools.partial(_a_gate_kernel, chunk=chunk, groups=groups),
        grid=(t // tm,),
        in_specs=[
            pl.BlockSpec((tm, d), lambda m: (m, 0)),
            pl.BlockSpec((tm, width), lambda m: (m, 0)),
            pl.BlockSpec((tm, width), lambda m: (m, 0)),
            pl.BlockSpec((1, width), lambda m: (0, 0)),
            pl.BlockSpec((None, groups, chunk, chunk), lambda m: (layer, 0, 0, 0)),
            pl.BlockSpec((chunk, groups), lambda m: (0, 0)),
            pl.BlockSpec((None, width, d), lambda m: (layer, 0, 0), pipeline_mode=pl.Buffered(1)),
        ],
        out_specs=pl.BlockSpec((tm, d), lambda m: (m, 0)),
        out_shape=jax.ShapeDtypeStruct((t, d), F32),
        scratch_shapes=[pltpu.VMEM((tm, width), BF16)],
        compiler_params=_params(("parallel",), vmem + (8 << 20)),
        name="a_gate_out_proj",
    )(h, u, v, v_gain.reshape(1, width), w_s, b_s.T, w_out)


def _ffn_kernel(*refs, tiles_per_seq, nj, final, cast_next):
    (h_ref, g_ref, wg_ref, wv_ref, cwg_ref, cwv_ref, cbg_ref, cbv_ref, wd_ref), refs = refs[:9], refs[9:]
    if final:
        fg_ref, refs = refs[0], refs[1:]
    if cast_next:
        (next_up_ref, next_down_ref), refs = refs[:2], refs[2:]
    o_ref, refs = refs[0], refs[1:]
    if cast_next:
        (next_up_out, next_down_out), refs = refs[:2], refs[2:]
    xn_ref, acc_ref, carry_ref, act0_ref, act1_ref, upg_ref, upv_ref, hbuf_ref, hsem = refs

    def cast_side_stream():
        if cast_next:
            next_up_out[...] = next_up_ref[...].astype(BF16)
            next_down_out[...] = next_down_ref[...].astype(BF16)
    acts = (act0_ref, act1_ref)
    m, j = pl.program_id(0), pl.program_id(1)
    tm, d = acc_ref.shape
    tn = wg_ref.shape[1]
    seq_start = (m % tiles_per_seq) == 0

    def residual_copy(tile):
        return pltpu.make_async_copy(h_ref.at[pl.ds(tile * tm, tm), :], hbuf_ref, hsem.at[0])

    @pl.when(j == 0)
    def _():
        @pl.when(m == 0)
        def _():
            residual_copy(0).start()

        residual_copy(m).wait()
        h = hbuf_ref[...]
        xn_ref[...] = _rmsnorm(h, g_ref[...]).astype(BF16)
        acc_ref[...] = h

    @pl.when(jnp.logical_and(j == 1, m + 1 < pl.num_programs(0)))
    def _():
        residual_copy(m + 1).start()

    @pl.when(jnp.logical_and(seq_start, j < nj))
    def _():
        carry_ref[2 * j] = jnp.zeros(carry_ref.shape[1:], F32)
        carry_ref[2 * j + 1] = jnp.zeros(carry_ref.shape[1:], F32)

    def up_dots():
        return _dot(xn_ref[...], wg_ref[...]), _dot(xn_ref[...], wv_ref[...])

    def conv(up, cw_ref, cb_ref, slot, up_ref):
        up_ref[:SUBLANES] = carry_ref[slot]
        up_ref[SUBLANES:] = up
        carry_ref[slot] = up[tm - SUBLANES:, :]
        w0, w1, w2, b = cw_ref[0:1], cw_ref[1:2], cw_ref[2:3], cb_ref[...]
        return (up * w2 + up_ref[SUBLANES - 1:SUBLANES - 1 + tm] * w1
                + up_ref[SUBLANES - 2:SUBLANES - 2 + tm] * w0 + b)

    def gate_act(ups, act_ref):
        gate = conv(ups[0], cwg_ref, cbg_ref, 2 * j, upg_ref)
        val = conv(ups[1], cwv_ref, cbv_ref, 2 * j + 1, upv_ref)
        act_ref[...] = (jax.nn.silu(gate) * val).astype(BF16)

    def down_proj(act_ref):
        acc_ref[...] += _dot(act_ref[...], wd_ref[...])

    @pl.when(j == 0)
    def _():
        cast_side_stream()
        gate_act(up_dots(), acts[0])

    for parity in range(2):
        @pl.when(jnp.logical_and(jnp.logical_and(j > 0, j < nj), j % 2 == parity))
        def _():
            cast_side_stream()
            ups = up_dots()
            down_proj(acts[1 - parity])
            gate_act(ups, acts[parity])

    @pl.when(j == nj)
    def _():
        cast_side_stream()
        down_proj(acts[(nj - 1) % 2])
        y = acc_ref[...]
        if final:
            y = _rmsnorm(y, fg_ref[...])
        o_ref[...] = y


def _ffn_tiles(t, seq_len, ffn, tiles):
    tm, tn = min(tiles.ffn_m, seq_len), min(tiles.ffn_n, ffn)
    assert seq_len % tm == 0 and ffn % tn == 0
    return tm, tn, t // tm, ffn // tn


def _can_cast_in_ffn(t, d, seq_len, ffn, tiles):
    _, _, nm, nj = _ffn_tiles(t, seq_len, ffn, tiles)
    return (d % nm == 0 and (d // nm) % LANES == 0 and (2 * ffn) % nj == 0
            and (2 * ffn // nj) % LANES == 0 and (ffn // nj) % (2 * SUBLANES) == 0)


def _ffn(h, gain, w_up, conv_w, conv_b, w_down, layer, seq_len, tiles, final_gain=None,
         next_weights=None):
    t, d = h.shape
    ffn = w_down.shape[0]
    tm, tn, nm, nj = _ffn_tiles(t, seq_len, ffn, tiles)
    final = final_gain is not None
    cast_next = next_weights is not None
    row = lambda m, j: (m, 0)
    const = lambda m, j: (0, 0)
    up_step = lambda j: jnp.minimum(j, nj - 1)
    in_specs = [
        pl.BlockSpec(memory_space=pl.ANY),
        pl.BlockSpec((1, d), const),
        pl.BlockSpec((d, tn), lambda m, j: (0, up_step(j))),
        pl.BlockSpec((d, tn), lambda m, j: (0, up_step(j) + nj)),
        pl.BlockSpec((None, conv_w.shape[1], tn), lambda m, j: (layer, 0, up_step(j))),
        pl.BlockSpec((None, conv_w.shape[1], tn), lambda m, j: (layer, 0, up_step(j) + nj)),
        pl.BlockSpec((None, 1, tn), lambda m, j: (layer, 0, up_step(j))),
        pl.BlockSpec((None, 1, tn), lambda m, j: (layer, 0, up_step(j) + nj)),
        pl.BlockSpec((tn, d), lambda m, j: (jnp.maximum(j - 1, 0), 0)),
    ]
    conv_b = conv_b.reshape(conv_b.shape[0], 1, -1)
    args = [h, gain.reshape(1, d), w_up, w_up, conv_w, conv_w, conv_b, conv_b, w_down]
    out_specs = [pl.BlockSpec((tm, d), row, pipeline_mode=pl.Buffered(1))]
    out_shape = [jax.ShapeDtypeStruct((t, d), F32)]
    if final:
        in_specs.append(pl.BlockSpec((1, d), const))
        args.append(final_gain.reshape(1, d))
    if cast_next:
        nxt_up, nxt_down, nxt = next_weights
        up_blk, down_blk = (d // nm, 2 * ffn // nj), (ffn // nj, d // nm)
        in_specs += [pl.BlockSpec((None,) + up_blk, lambda m, j: (nxt, m, up_step(j))),
                     pl.BlockSpec((None,) + down_blk, lambda m, j: (nxt, up_step(j), m))]
        out_specs += [pl.BlockSpec(up_blk, lambda m, j: (m, up_step(j))),
                      pl.BlockSpec(down_blk, lambda m, j: (up_step(j), m))]
        out_shape += [jax.ShapeDtypeStruct((d, 2 * ffn), BF16),
                      jax.ShapeDtypeStruct((ffn, d), BF16)]
        args += [nxt_up, nxt_down]
    vmem = (2 * tm * d * 4 + tm * d * 2 + tm * d * 4 + 2 * 3 * d * tn * 2 + 2 * tm * tn * 2
            + 8 * tm * tn * 4 + (2 * 6 * 3 * d * ffn // (nm * nj) if cast_next else 0))
    outs = pl.pallas_call(
        functools.partial(_ffn_kernel, tiles_per_seq=seq_len // tm, nj=nj, final=final,
                          cast_next=cast_next),
        grid=(nm, nj + 1),
        in_specs=in_specs,
        out_specs=out_specs,
        out_shape=out_shape,
        scratch_shapes=[
            pltpu.VMEM((tm, d), BF16),
            pltpu.VMEM((tm, d), F32),
            pltpu.VMEM((2 * nj, SUBLANES, tn), F32),
            pltpu.VMEM((tm, tn), BF16),
            pltpu.VMEM((tm, tn), BF16),
            pltpu.VMEM((SUBLANES + tm, tn), F32),
            pltpu.VMEM((SUBLANES + tm, tn), F32),
            pltpu.VMEM((tm, d), F32),
            pltpu.SemaphoreType.DMA((1,)),
        ],
        compiler_params=_params(("arbitrary", "arbitrary"), vmem + (8 << 20)),
        name="conv_ffn",
    )(*args)
    return outs if cast_next else outs[0]


def _kv_kernel(h_ref, g_ref, wkt_ref, wv_ref, wf_ref, kn_ref, kt_ref, v_ref, f_ref, xn_ref,
               *, head_dim):
    _store_xn(xn_ref, h_ref, g_ref)
    xn = xn_ref[...]
    kt = lax.dot_general(wkt_ref[...], xn, (((1,), (1,)), ((), ())), preferred_element_type=F32)
    for i in range(kt.shape[0] // head_dim):
        rows = slice(i * head_dim, (i + 1) * head_dim)
        x = kt[rows]
        r = lax.rsqrt(jnp.mean(x * x, axis=0, keepdims=True) + EPS)
        kt_ref[rows, :] = ((x * r) * kn_ref[...]).astype(BF16)
    v_ref[...] = _dot(xn, wv_ref[...]).astype(BF16)

    @pl.when(pl.program_id(1) == 0)
    def _():
        f_ref[...] = _dot(xn, wf_ref[...])


def _kv_proj(h, gain, w_kt, w_kvf, w_f, k_gain, tiles):
    t, d = h.shape
    width = w_kt.shape[0]
    head_dim = k_gain.shape[0]
    tm, tn = min(tiles.proj_m, t), min(tiles.proj_n, width)
    v_first = width // tn
    vmem = (2 * tm * d * 4 + tm * d * 2 + 2 * 2 * d * tn * 2 + 2 * d * LANES * 2
            + 2 * 2 * tm * tn * 2 + 2 * tm * LANES * 4 + 6 * tm * tn * 4)
    return pl.pallas_call(
        functools.partial(_kv_kernel, head_dim=head_dim),
        grid=(t // tm, width // tn),
        in_specs=[
            pl.BlockSpec((tm, d), lambda m, n: (m, 0)),
            pl.BlockSpec((1, d), lambda m, n: (0, 0)),
            pl.BlockSpec((tn, d), lambda m, n: (n, 0)),
            pl.BlockSpec((d, tn), lambda m, n: (0, n + v_first)),
            pl.BlockSpec((d, LANES), lambda m, n: (0, 0)),
            pl.BlockSpec((head_dim, 1), lambda m, n: (0, 0)),
        ],
        out_specs=[
            pl.BlockSpec((tn, tm), lambda m, n: (n, m)),
            pl.BlockSpec((tm, tn), lambda m, n: (m, n)),
            pl.BlockSpec((tm, LANES), lambda m, n: (m, 0)),
        ],
        out_shape=[
            jax.ShapeDtypeStruct((width, t), BF16),
            jax.ShapeDtypeStruct((t, width), BF16),
            jax.ShapeDtypeStruct((t, LANES), F32),
        ],
        scratch_shapes=[pltpu.VMEM((tm, d), BF16)],
        compiler_params=_params(("parallel", "arbitrary"), vmem + (8 << 20)),
        name="kv_proj",
    )(h, gain.reshape(1, d), w_kt, w_kvf, w_f, k_gain.reshape(head_dim, 1))


def _forget_cumsum_kernel(f_ref, b_ref, c_ref):
    rows, seq = f_ref.shape
    lane = lax.broadcasted_iota(jnp.int32, (rows, LANES), 1)
    carry = jnp.zeros((rows, 1), F32)
    for blk in range(seq // LANES):
        cols = slice(blk * LANES, (blk + 1) * LANES)
        x = _log_sigmoid(f_ref[:, cols] + b_ref[...])
        d = 1
        while d < LANES:
            x = x + jnp.where(lane >= d, pltpu.roll(x, d, axis=1), 0.0)
            d *= 2
        x = x + carry
        c_ref[:, cols] = x
        carry = x[:, LANES - 1:LANES]


def _forget_cumsum(f_rows, b_rows):
    rows, seq = f_rows.shape
    return pl.pallas_call(
        _forget_cumsum_kernel,
        out_shape=jax.ShapeDtypeStruct((rows, seq), F32),
        name="forget_cumsum",
    )(f_rows, b_rows)


def _qg_kernel(h_ref, g_ref, wq_ref, wg_ref, qn_ref, q_ref, sg_ref, xn_ref, *, head_dim):
    _store_xn(xn_ref, h_ref, g_ref)
    xn = xn_ref[...]
    _head_rmsnorm_store(q_ref, _dot(xn, wq_ref[...]), qn_ref[...], head_dim,
                        scale=head_dim ** -0.5 * LOG2E)
    sg_ref[...] = jax.nn.sigmoid(_dot(xn, wg_ref[...])).astype(BF16)


def _qg_proj(h, gain, w_qg, q_gain, layer, tiles):
    t, d = h.shape
    width = w_qg.shape[2] // 2
    head_dim = q_gain.shape[0]
    tm, tn = min(tiles.proj_m, t), min(tiles.proj_n, width)
    nn = width // tn
    vmem = 2 * tm * d * 4 + tm * d * 2 + 2 * 2 * d * tn * 2 + 2 * 2 * tm * tn * 2 + 6 * tm * tn * 4
    return pl.pallas_call(
        functools.partial(_qg_kernel, head_dim=head_dim),
        grid=(t // tm, nn),
        in_specs=[
            pl.BlockSpec((tm, d), lambda m, n: (m, 0)),
            pl.BlockSpec((1, d), lambda m, n: (0, 0)),
            pl.BlockSpec((None, d, tn), lambda m, n: (layer, 0, n)),
            pl.BlockSpec((None, d, tn), lambda m, n: (layer, 0, n + nn)),
            pl.BlockSpec((1, head_dim), lambda m, n: (0, 0)),
        ],
        out_specs=[pl.BlockSpec((tm, tn), lambda m, n: (m, n))] * 2,
        out_shape=[jax.ShapeDtypeStruct((t, width), BF16)] * 2,
        scratch_shapes=[pltpu.VMEM((tm, d), BF16)],
        compiler_params=_params(("parallel", "arbitrary"), vmem + (8 << 20)),
        name="qg_proj",
    )(h, gain.reshape(1, d), w_qg, w_qg, q_gain.reshape(1, head_dim))


C_TERMS = 3


def _split_bf16_terms(x):
    terms = []
    for _ in range(C_TERMS):
        t = x.astype(BF16).astype(F32)
        terms.append(t)
        x = x - t
    return terms


def _attn_kernel(q_ref, kt_ref, v_ref, crow_ref, sg_ref, o_ref,
                 qa_ref, kta_ref, va_ref, m_ref, acc_ref, *, heads, head_dim):
    blk = q_ref.shape[0]
    nk = kta_ref.shape[1]
    hd = head_dim
    qi = pl.program_id(2)

    @pl.when(qi == 0)
    def _():
        sub = lax.broadcasted_iota(jnp.int32, (2 * SUBLANES, blk), 0)
        for h in range(heads):
            for kb in range(nk):
                cols = slice(kb * blk, (kb + 1) * blk)
                c_s = _split_bf16_terms(crow_ref[h, kb] * LOG2E)
                tail = jnp.where(sub < C_TERMS, 1.0, 0.0)
                for j, term in enumerate(c_s):
                    tail = jnp.where(sub == C_TERMS + j, -term, tail)
                kta_ref[h, kb, :hd, :] = kt_ref[h * hd:(h + 1) * hd, cols]
                kta_ref[h, kb, hd:hd + 2 * SUBLANES, :] = tail.astype(BF16)
                kta_ref[h, kb, hd + 2 * SUBLANES:, :] = jnp.zeros((hd - 2 * SUBLANES, blk), BF16)
            va_ref[h, :, :hd] = v_ref[:, h * hd:(h + 1) * hd]
            va_ref[h, :, hd:] = jnp.ones((v_ref.shape[0], hd), BF16)

    sub = lax.broadcasted_iota(jnp.int32, (hd, hd), 0)
    for h in range(heads):
        c_t = _split_bf16_terms(crow_ref[h, qi] * LOG2E)
        qa_ref[h, :, :hd] = q_ref[:, h * hd:(h + 1) * hd]
        for rb in range(blk // hd):
            rows = slice(rb * hd, (rb + 1) * hd)
            tail_t = jnp.where(jnp.logical_and(sub >= C_TERMS, sub < 2 * C_TERMS), 1.0, 0.0)
            for j, term in enumerate(c_t):
                tail_t = jnp.where(sub == j, term[:, rows], tail_t)
            qa_ref[h, rows, hd:] = tail_t.T.astype(BF16)

    def blocks(k0, count, diagonal):
        start = pl.multiple_of(k0 * blk, blk)
        scores = []
        for h in range(heads):
            chunks = []
            for b in range(count):
                s = _dot(qa_ref[h], kta_ref[h, k0 + b])
                if diagonal:
                    row = lax.broadcasted_iota(jnp.int32, (blk, blk), 0)
                    col = lax.broadcasted_iota(jnp.int32, (blk, blk), 1)
                    s = jnp.where(row >= col, s, -jnp.inf)
                chunks += [s[:, c * LANES:(c + 1) * LANES] for c in range(blk // LANES)]
            scores.append(chunks)
        probs = []
        for h, chunks in enumerate(scores):
            m_new = jnp.max(functools.reduce(jnp.maximum, chunks), axis=1, keepdims=True)
            if diagonal:
                m_new = jnp.broadcast_to(m_new, (blk, LANES))
                alpha = None
            else:
                m_old = m_ref[h]
                m_new = jnp.maximum(m_old, m_new)
                alpha = jnp.exp2(m_old - m_new)
            m_ref[h] = m_new
            p = jnp.concatenate([jnp.exp2(c - m_new).astype(BF16) for c in chunks], axis=1)
            probs.append((p, alpha))
        for h, (p, alpha) in enumerate(probs):
            pv = _dot(p, va_ref[h, pl.ds(start, count * blk), :])
            if diagonal:
                acc_ref[h] = pv
            else:
                acc_ref[h] = jnp.concatenate([alpha] * (2 * hd // LANES), axis=1) * acc_ref[h] + pv

    blocks(qi, 1, True)
    odd = qi % 2

    @pl.when(odd == 1)
    def _():
        blocks(0, 1, False)

    def pair(i, carry):
        blocks(2 * i + odd, 2, False)
        return carry

    lax.fori_loop(0, qi // 2, pair, 0)
    for h in range(heads):
        acc = acc_ref[h]
        o = acc[:, :hd] / acc[:, hd:]
        cols = slice(h * hd, (h + 1) * hd)
        o_ref[:, cols] = (o * sg_ref[:, cols].astype(F32)).astype(o_ref.dtype)


def _attention(q, kt, v, c, sg, batch, heads, tiles):
    t, width = q.shape
    seq = t // batch
    hd = width // heads
    assert hd == LANES
    blk = min(tiles.attn, seq)
    nq = seq // blk
    hp = min(tiles.attn_heads, heads)
    c_row = c.reshape(batch * heads, nq, 1, blk)
    qmap = lambda b, g, i: (b * nq + i, g)
    vmem = (2 * 2 * seq * hp * hd * 2 + 2 * seq * hp * 2 * hd * 2 + 2 * 3 * blk * hp * hd * 2
            + hp * blk * (2 * hd * 2 + LANES * 4 + 2 * hd * 4) + 2 * hp * seq * 4 * SUBLANES
            + 2 * hp * blk * LANES * 4 + hp * 3 * blk * blk * 4)
    return pl.pallas_call(
        functools.partial(_attn_kernel, heads=hp, head_dim=hd),
        grid=(batch, heads // hp, nq),
        in_specs=[
            pl.BlockSpec((blk, hp * hd), qmap),
            pl.BlockSpec((hp * hd, seq), lambda b, g, i: (g, b)),
            pl.BlockSpec((seq, hp * hd), lambda b, g, i: (b, g)),
            pl.BlockSpec((hp, nq, 1, blk), lambda b, g, i: (b * (heads // hp) + g, 0, 0, 0)),
            pl.BlockSpec((blk, hp * hd), qmap),
        ],
        out_specs=pl.BlockSpec((blk, hp * hd), qmap),
        out_shape=jax.ShapeDtypeStruct((t, width), BF16),
        scratch_shapes=[
            pltpu.VMEM((hp, blk, 2 * hd), BF16),
            pltpu.VMEM((hp, nq, 2 * hd, blk), BF16),
            pltpu.VMEM((hp, seq, 2 * hd), BF16),
            pltpu.VMEM((hp, blk, LANES), F32),
            pltpu.VMEM((hp, blk, 2 * hd), F32),
        ],
        compiler_params=_params(("parallel", "parallel", "arbitrary"), vmem + (8 << 20)),
        name="fox_attention",
    )(q, kt, v, c_row, sg)


def _out_proj_kernel(h_ref, a_ref, w_ref, o_ref):
    o_ref[...] = h_ref[...] + _dot(a_ref[...], w_ref[...])


def _out_proj(h, a, w, layer, tiles):
    t, d = h.shape
    kdim = a.shape[1]
    tm = min(tiles.gate_m, t)
    vmem = 4 * tm * d * 4 + 2 * tm * kdim * 2 + kdim * d * 2 + 2 * tm * d * 4
    return pl.pallas_call(
        _out_proj_kernel,
        grid=(t // tm,),
        in_specs=[
            pl.BlockSpec((tm, d), lambda m: (m, 0)),
            pl.BlockSpec((tm, kdim), lambda m: (m, 0)),
            pl.BlockSpec((None, kdim, d), lambda m: (layer, 0, 0), pipeline_mode=pl.Buffered(1)),
        ],
        out_specs=pl.BlockSpec((tm, d), lambda m: (m, 0)),
        out_shape=jax.ShapeDtypeStruct((t, d), F32),
        compiler_params=_params(("parallel",), vmem + (8 << 20)),
        name="b_out_proj",
    )(h, a, w)


def _forward(x, a_norm, a_w_in, a_v_norm, a_w_s, a_b_s, a_w_out, kv_norm, w_kvf, b_f, k_norm,
             b_norm, b_w_qg, q_norm, b_w_out, f_norm, f_w_up, f_conv_w, f_conv_b, f_w_down,
             final_norm, tiles=Tiles()):
    batch, seq, d = x.shape
    depth = f_norm.shape[0]
    n_a = a_norm.shape[0]
    heads = b_f.shape[0]
    att = heads * k_norm.shape[0]
    h = x.reshape(batch * seq, d)
    bf = lambda w: w.astype(BF16)
    a_w_in, a_w_s, a_w_out = bf(a_w_in), bf(a_w_s), bf(a_w_out)
    b_w_qg, b_w_out, w_kvf = bf(b_w_qg), bf(b_w_out), bf(w_kvf)
    ffn_dim = f_w_down.shape[1]
    cast_in_ffn = _can_cast_in_ffn(batch * seq, d, seq, ffn_dim, tiles)
    w_up, w_down = bf(f_w_up[0]), bf(f_w_down[0])
    k = v = c = None
    for l in range(depth):
        if l < n_a:
            u, vv = _a_in(h, a_norm[l], a_w_in, l, tiles)
            h = _a_gate(h, u, vv, a_v_norm[l], a_w_s, a_b_s[l], a_w_out, l, tiles)
        else:
            j = l - n_a
            q, sg = _qg_proj(h, b_norm[j], b_w_qg, q_norm[j], j, tiles)
            og = _attention(q, k, v, c, sg, batch, heads, tiles)
            h = _out_proj(h, og, b_w_out, j, tiles)
        if l == depth - 1:
            h = _ffn(h, f_norm[l], w_up, f_conv_w, f_conv_b, w_down, l, seq, tiles,
                     final_gain=final_norm)
        elif cast_in_ffn:
            h, w_up, w_down = _ffn(h, f_norm[l], w_up, f_conv_w, f_conv_b, w_down, l, seq, tiles,
                                   next_weights=(f_w_up, f_w_down, l + 1))
        else:
            h = _ffn(h, f_norm[l], w_up, f_conv_w, f_conv_b, w_down, l, seq, tiles)
            w_up, w_down = bf(f_w_up[l + 1]), bf(f_w_down[l + 1])
        if l == n_a - 1:
            w_f = jnp.pad(w_kvf[:, 2 * att:], ((0, 0), (0, LANES - heads)))
            k, v, f = _kv_proj(h, kv_norm, w_kvf[:, :att].T, w_kvf, w_f, k_norm, tiles)
            f_rows = f[:, :heads].reshape(batch, seq, heads).transpose(0, 2, 1)
            c = _forget_cumsum(f_rows.reshape(batch * heads, seq),
                               jnp.tile(b_f, batch).reshape(batch * heads, 1))
    return h.reshape(batch, seq, d)


def kernel(x, a_norm, a_w_in, a_v_norm, a_w_s, a_b_s, a_w_out, kv_norm, w_kvf, b_f, k_norm,
           b_norm, b_w_qg, q_norm, b_w_out, f_norm, f_w_up, f_conv_w, f_conv_b, f_w_down,
           final_norm):
    return _forward(x, a_norm, a_w_in, a_v_norm, a_w_s, a_b_s, a_w_out, kv_norm, w_kvf, b_f,
                    k_norm, b_norm, b_w_qg, q_norm, b_w_out, f_norm, f_w_up, f_conv_w, f_conv_b,
                    f_w_down, final_norm)
```

```python
import functools
from typing import NamedTuple

import jax
import jax.numpy as jnp
from jax import lax
from jax.experimental import pallas as pl
from jax.experimental.pallas import tpu as pltpu

F32 = jnp.float32
BF16 = jnp.bfloat16
EPS = 1e-6
LOG2E = 1.4426950408889634
SUBLANES = 8
LANES = 128
V7X_VMEM_BYTES = 64 * 1024 * 1024
VMEM_RESERVE = 4 * 1024 * 1024
VMEM_SLACK = 8 * 1024 * 1024


class Tiles(NamedTuple):
    proj_m: int = 1024
    proj_n: int = 1024
    gate_m: int = 512
    ffn_m: int = 1024
    ffn_n: int = 512
    attn: int = 512
    attn_heads: int = 4


def _params(semantics, vmem_estimate):
    limit = min(vmem_estimate + VMEM_SLACK, V7X_VMEM_BYTES - VMEM_RESERVE)
    return pltpu.CompilerParams(dimension_semantics=semantics, vmem_limit_bytes=limit)


def _rmsnorm(x, gain):
    r = lax.rsqrt(jnp.mean(x * x, axis=-1, keepdims=True) + EPS)
    return (x * r) * gain


def _head_rmsnorm_store(o_ref, x, gain, head_dim, scale):
    for i in range(x.shape[1] // head_dim):
        xh = x[:, i * head_dim:(i + 1) * head_dim]
        o_ref[:, i * head_dim:(i + 1) * head_dim] = (_rmsnorm(xh, gain) * scale).astype(o_ref.dtype)


def _gelu(x):
    return 0.5 * x * (1.0 + lax.erf(x * (0.5 ** 0.5)))


def _log_sigmoid(x):
    return jnp.minimum(x, 0.0) - jnp.log1p(jnp.exp(-jnp.abs(x)))


def _dot(a, b):
    return jnp.dot(a, b, preferred_element_type=F32)


def _store_xn(xn_ref, h_ref, g_ref):
    @pl.when(pl.program_id(1) == 0)
    def _():
        xn_ref[...] = _rmsnorm(h_ref[...], g_ref[...]).astype(BF16)


def _side_cast_block(shape, row_blocks, col_blocks):
    rows, cols = shape
    if rows % row_blocks or cols % col_blocks:
        return None
    blk = (rows // row_blocks, cols // col_blocks)
    return blk if blk[0] % (2 * SUBLANES) == 0 and blk[1] % LANES == 0 else None


def _a_in_kernel(*refs, cast):
    h_ref, g_ref, wu_ref, wv_ref = refs[:4]
    if cast:
        cast_in_ref, u_ref, v_ref, cast_out_ref, xn_ref = refs[4:]
    else:
        u_ref, v_ref, xn_ref = refs[4:]
    _store_xn(xn_ref, h_ref, g_ref)
    xn = xn_ref[...]
    if cast:
        cast_out_ref[...] = cast_in_ref[...].astype(BF16)
    u_ref[...] = _gelu(_dot(xn, wu_ref[...])).astype(BF16)
    v_ref[...] = _gelu(_dot(xn, wv_ref[...])).astype(BF16)


def _a_in(h, gain, w_in, layer, tiles, cast=None):
    t, d = h.shape
    width = w_in.shape[2] // 2
    tm, tn = min(tiles.proj_m, t), min(tiles.proj_n, width)
    nm, nn = t // tm, width // tn
    vmem = 2 * tm * d * 4 + tm * d * 2 + 2 * 2 * d * tn * 2 + 2 * 2 * tm * tn * 2 + 6 * tm * tn * 4
    in_specs = [
        pl.BlockSpec((tm, d), lambda m, n: (m, 0)),
        pl.BlockSpec((1, d), lambda m, n: (0, 0)),
        pl.BlockSpec((None, d, tn), lambda m, n: (layer, 0, n)),
        pl.BlockSpec((None, d, tn), lambda m, n: (layer, 0, n + nn)),
    ]
    out_specs = [pl.BlockSpec((tm, tn), lambda m, n: (m, n))] * 2
    out_shape = [jax.ShapeDtypeStruct((t, width), BF16)] * 2
    args = [h, gain.reshape(1, d), w_in, w_in]
    if cast is not None:
        w_cast, cast_layer = cast
        blk = _side_cast_block(w_cast.shape[1:], nm, nn)
        in_specs.append(pl.BlockSpec((None,) + blk, lambda m, n: (cast_layer, m, n)))
        out_specs.append(pl.BlockSpec(blk, lambda m, n: (m, n)))
        out_shape.append(jax.ShapeDtypeStruct(w_cast.shape[1:], BF16))
        args.append(w_cast)
        vmem += 2 * 6 * blk[0] * blk[1]
    return pl.pallas_call(
        functools.partial(_a_in_kernel, cast=cast is not None),
        grid=(nm, nn),
        in_specs=in_specs,
        out_specs=out_specs,
        out_shape=out_shape,
        scratch_shapes=[pltpu.VMEM((tm, d), BF16)],
        compiler_params=_params(("parallel", "arbitrary"), vmem),
        name="a_in_proj",
    )(*args)


def _a_gate_kernel(*refs, chunk, groups, cast):
    h_ref, u_ref, v_ref, gn_ref, ws_ref, bt_ref, wo_ref = refs[:7]
    if cast:
        cast_in_ref, o_ref, cast_out_ref, gated_ref = refs[7:]
        cast_out_ref[...] = cast_in_ref[...].astype(BF16)
    else:
        o_ref, gated_ref = refs[7:]
    tm, width = v_ref.shape
    gdim = width // groups
    row = lax.broadcasted_iota(jnp.int32, (chunk, chunk), 0)
    col = lax.broadcasted_iota(jnp.int32, (chunk, chunk), 1)
    causal = row >= col
    parts = 2 if (tm // chunk) % 2 == 0 else 1
    nchunks = tm // chunk // parts
    for part in range(parts):
        base = part * nchunks * chunk
        part_rows = slice(base, base + nchunks * chunk)
        vs = _rmsnorm(v_ref[part_rows, :].astype(F32), gn_ref[...]).astype(BF16)
        for g in range(groups):
            w = jnp.where(causal, ws_ref[g], jnp.zeros((), BF16))
            bias = bt_ref[:, g:g + 1]
            cols = slice(g * gdim, (g + 1) * gdim)
            rhs = jnp.concatenate([vs[c * chunk:(c + 1) * chunk, cols] for c in range(nchunks)],
                                  axis=1)
            mixed = _dot(w, rhs) + bias
            for c in range(nchunks):
                rows = slice(base + c * chunk, base + (c + 1) * chunk)
                gated_ref[rows, cols] = (u_ref[rows, cols].astype(F32)
                                         * mixed[:, c * gdim:(c + 1) * gdim]).astype(BF16)
        o_ref[part_rows, :] = h_ref[part_rows, :] + _dot(gated_ref[part_rows, :], wo_ref[...])


def _a_gate(h, u, v, v_gain, w_s, b_s, w_out, layer, tiles, cast=None):
    t, d = h.shape
    width = u.shape[1]
    _, groups, chunk, _ = w_s.shape
    tm = min(tiles.gate_m, t)
    vmem = (4 * tm * d * 4 + 4 * tm * width * 2 + 2 * width * d * 2 + tm * width * 2
            + 3 * tm * width * 4 + tm * d * 4)
    in_specs = [
        pl.BlockSpec((tm, d), lambda m: (m, 0)),
        pl.BlockSpec((tm, width), lambda m: (m, 0)),
        pl.BlockSpec((tm, width), lambda m: (m, 0)),
        pl.BlockSpec((1, width), lambda m: (0, 0)),
        pl.BlockSpec((None, groups, chunk, chunk), lambda m: (layer, 0, 0, 0)),
        pl.BlockSpec((chunk, groups), lambda m: (0, 0)),
        pl.BlockSpec((None, width, d), lambda m: (layer, 0, 0), pipeline_mode=pl.Buffered(1)),
    ]
    out_specs = [pl.BlockSpec((tm, d), lambda m: (m, 0))]
    out_shape = [jax.ShapeDtypeStruct((t, d), F32)]
    args = [h, u, v, v_gain.reshape(1, width), w_s, b_s.T, w_out]
    if cast is not None:
        w_cast, cast_layer = cast
        blk = _side_cast_block(w_cast.shape[1:], t // tm, 1)
        in_specs.append(pl.BlockSpec((None,) + blk, lambda m: (cast_layer, m, 0)))
        out_specs.append(pl.BlockSpec(blk, lambda m: (m, 0)))
        out_shape.append(jax.ShapeDtypeStruct(w_cast.shape[1:], BF16))
        args.append(w_cast)
        vmem += 2 * 6 * blk[0] * blk[1]
    outs = pl.pallas_call(
        functools.partial(_a_gate_kernel, chunk=chunk, groups=groups, cast=cast is not None),
        grid=(t // tm,),
        in_specs=in_specs,
        out_specs=out_specs,
        out_shape=out_shape,
        scratch_shapes=[pltpu.VMEM((tm, width), BF16)],
        compiler_params=_params(("parallel",), vmem),
        name="a_gate_out_proj",
    )(*args)
    return outs if cast is not None else outs[0]


def _ffn_kernel(*refs, tiles_per_seq, nj, final, cast_next):
    (h_ref, g_ref, wg_ref, wv_ref, cwg_ref, cwv_ref, cbg_ref, cbv_ref, wd_ref), refs = refs[:9], refs[9:]
    if final:
        fg_ref, refs = refs[0], refs[1:]
    if cast_next:
        (next_up_ref, next_down_ref), refs = refs[:2], refs[2:]
    o_ref, refs = refs[0], refs[1:]
    if cast_next:
        (next_up_out, next_down_out), refs = refs[:2], refs[2:]
    (xn_ref, acc_ref, carry_ref, act0_ref, act1_ref, upg_ref, upv_ref,
     hbuf_ref, obuf_ref, hsem, osem) = refs

    def cast_side_stream():
        if cast_next:
            next_up_out[...] = next_up_ref[...].astype(BF16)
            next_down_out[...] = next_down_ref[...].astype(BF16)
    acts = (act0_ref, act1_ref)
    m, j = pl.program_id(0), pl.program_id(1)
    tm, d = acc_ref.shape
    tn = wg_ref.shape[1]
    seq_start = (m % tiles_per_seq) == 0

    def residual_copy(tile):
        return pltpu.make_async_copy(h_ref.at[pl.ds(tile * tm, tm), :], hbuf_ref, hsem.at[0])

    @pl.when(j == 0)
    def _():
        @pl.when(m == 0)
        def _():
            residual_copy(0).start()

        residual_copy(m).wait()
        h = hbuf_ref[...]
        xn_ref[...] = _rmsnorm(h, g_ref[...]).astype(BF16)
        acc_ref[...] = h

    @pl.when(jnp.logical_and(j == 1, m + 1 < pl.num_programs(0)))
    def _():
        residual_copy(m + 1).start()

    @pl.when(jnp.logical_and(seq_start, j < nj))
    def _():
        carry_ref[2 * j] = jnp.zeros(carry_ref.shape[1:], F32)
        carry_ref[2 * j + 1] = jnp.zeros(carry_ref.shape[1:], F32)

    def up_dots():
        return _dot(xn_ref[...], wg_ref[...]), _dot(xn_ref[...], wv_ref[...])

    def conv(up, cw_ref, cb_ref, slot, up_ref):
        up_ref[:SUBLANES] = carry_ref[slot]
        up_ref[SUBLANES:] = up
        carry_ref[slot] = up[tm - SUBLANES:, :]
        w0, w1, w2, b = cw_ref[0:1], cw_ref[1:2], cw_ref[2:3], cb_ref[...]
        return (up * w2 + up_ref[SUBLANES - 1:SUBLANES - 1 + tm] * w1
                + up_ref[SUBLANES - 2:SUBLANES - 2 + tm] * w0 + b)

    def gate_act(ups, act_ref):
        gate = conv(ups[0], cwg_ref, cbg_ref, 2 * j, upg_ref)
        val = conv(ups[1], cwv_ref, cbv_ref, 2 * j + 1, upv_ref)
        act_ref[...] = (jax.nn.silu(gate) * val).astype(BF16)

    def down_proj(act_ref):
        acc_ref[...] += _dot(act_ref[...], wd_ref[...])

    @pl.when(j == 0)
    def _():
        cast_side_stream()
        gate_act(up_dots(), acts[0])

    for parity in range(2):
        @pl.when(jnp.logical_and(jnp.logical_and(j > 0, j < nj), j % 2 == parity))
        def _():
            cast_side_stream()
            ups = up_dots()
            down_proj(acts[1 - parity])
            gate_act(ups, acts[parity])

    def output_copy(tile):
        return pltpu.make_async_copy(obuf_ref, o_ref.at[pl.ds(tile * tm, tm), :], osem.at[0])

    @pl.when(j == nj)
    def _():
        cast_side_stream()
        down_proj(acts[(nj - 1) % 2])
        y = acc_ref[...]
        if final:
            y = _rmsnorm(y, fg_ref[...])

        @pl.when(m > 0)
        def _():
            output_copy(m - 1).wait()

        obuf_ref[...] = y
        output_copy(m).start()

        @pl.when(m + 1 == pl.num_programs(0))
        def _():
            output_copy(m).wait()


def _ffn_tiles(t, seq_len, ffn, tiles):
    tm, tn = min(tiles.ffn_m, seq_len), min(tiles.ffn_n, ffn)
    assert seq_len % tm == 0 and ffn % tn == 0
    return tm, tn, t // tm, ffn // tn


def _can_cast_in_ffn(t, d, seq_len, ffn, tiles):
    _, _, nm, nj = _ffn_tiles(t, seq_len, ffn, tiles)
    return (d % nm == 0 and (d // nm) % LANES == 0 and (2 * ffn) % nj == 0
            and (2 * ffn // nj) % LANES == 0 and (ffn // nj) % (2 * SUBLANES) == 0)


def _ffn(h, gain, w_up, conv_w, conv_b, w_down, layer, seq_len, tiles, final_gain=None,
         next_weights=None):
    t, d = h.shape
    ffn = w_down.shape[0]
    tm, tn, nm, nj = _ffn_tiles(t, seq_len, ffn, tiles)
    final = final_gain is not None
    cast_next = next_weights is not None
    const = lambda m, j: (0, 0)
    up_step = lambda j: jnp.minimum(j, nj - 1)
    in_specs = [
        pl.BlockSpec(memory_space=pl.ANY),
        pl.BlockSpec((1, d), const),
        pl.BlockSpec((d, tn), lambda m, j: (0, up_step(j))),
        pl.BlockSpec((d, tn), lambda m, j: (0, up_step(j) + nj)),
        pl.BlockSpec((None, conv_w.shape[1], tn), lambda m, j: (layer, 0, up_step(j))),
        pl.BlockSpec((None, conv_w.shape[1], tn), lambda m, j: (layer, 0, up_step(j) + nj)),
        pl.BlockSpec((None, 1, tn), lambda m, j: (layer, 0, up_step(j))),
        pl.BlockSpec((None, 1, tn), lambda m, j: (layer, 0, up_step(j) + nj)),
        pl.BlockSpec((tn, d), lambda m, j: (jnp.maximum(j - 1, 0), 0)),
    ]
    conv_b = conv_b.reshape(conv_b.shape[0], 1, -1)
    args = [h, gain.reshape(1, d), w_up, w_up, conv_w, conv_w, conv_b, conv_b, w_down]
    out_specs = [pl.BlockSpec(memory_space=pl.ANY)]
    out_shape = [jax.ShapeDtypeStruct((t, d), F32)]
    if final:
        in_specs.append(pl.BlockSpec((1, d), const))
        args.append(final_gain.reshape(1, d))
    if cast_next:
        nxt_up, nxt_down, nxt = next_weights
        up_blk, down_blk = (d // nm, 2 * ffn // nj), (ffn // nj, d // nm)
        in_specs += [pl.BlockSpec((None,) + up_blk, lambda m, j: (nxt, m, up_step(j))),
                     pl.BlockSpec((None,) + down_blk, lambda m, j: (nxt, up_step(j), m))]
        out_specs += [pl.BlockSpec(up_blk, lambda m, j: (m, up_step(j))),
                      pl.BlockSpec(down_blk, lambda m, j: (up_step(j), m))]
        out_shape += [jax.ShapeDtypeStruct((d, 2 * ffn), BF16),
                      jax.ShapeDtypeStruct((ffn, d), BF16)]
        args += [nxt_up, nxt_down]
    vmem = (2 * tm * d * 4 + tm * d * 2 + tm * d * 4 + 2 * 3 * d * tn * 2 + 2 * tm * tn * 2
            + 8 * tm * tn * 4 + (2 * 6 * 3 * d * ffn // (nm * nj) if cast_next else 0))
    outs = pl.pallas_call(
        functools.partial(_ffn_kernel, tiles_per_seq=seq_len // tm, nj=nj, final=final,
                          cast_next=cast_next),
        grid=(nm, nj + 1),
        in_specs=in_specs,
        out_specs=out_specs,
        out_shape=out_shape,
        scratch_shapes=[
            pltpu.VMEM((tm, d), BF16),
            pltpu.VMEM((tm, d), F32),
            pltpu.VMEM((2 * nj, SUBLANES, tn), F32),
            pltpu.VMEM((tm, tn), BF16),
            pltpu.VMEM((tm, tn), BF16),
            pltpu.VMEM((SUBLANES + tm, tn), F32),
            pltpu.VMEM((SUBLANES + tm, tn), F32),
            pltpu.VMEM((tm, d), F32),
            pltpu.VMEM((tm, d), F32),
            pltpu.SemaphoreType.DMA((1,)),
            pltpu.SemaphoreType.DMA((1,)),
        ],
        compiler_params=_params(("arbitrary", "arbitrary"), vmem),
        name="conv_ffn",
    )(*args)
    return outs if cast_next else outs[0]


def _kv_kernel(h_ref, g_ref, wkt_ref, wv_ref, wf_ref, kn_ref, kt_ref, v_ref, f_ref, xn_ref,
               *, head_dim):
    _store_xn(xn_ref, h_ref, g_ref)
    xn = xn_ref[...]
    kt = lax.dot_general(wkt_ref[...], xn, (((1,), (1,)), ((), ())), preferred_element_type=F32)
    for i in range(kt.shape[0] // head_dim):
        rows = slice(i * head_dim, (i + 1) * head_dim)
        x = kt[rows]
        r = lax.rsqrt(jnp.mean(x * x, axis=0, keepdims=True) + EPS)
        kt_ref[rows, :] = ((x * r) * kn_ref[...]).astype(BF16)
    v_ref[...] = _dot(xn, wv_ref[...]).astype(BF16)

    @pl.when(pl.program_id(1) == 0)
    def _():
        f_ref[...] = _dot(xn, wf_ref[...])


def _kv_proj(h, gain, w_kt, w_kvf, w_f, k_gain, tiles):
    t, d = h.shape
    width = w_kt.shape[0]
    head_dim = k_gain.shape[0]
    tm, tn = min(tiles.proj_m, t), min(tiles.proj_n, width)
    v_first = width // tn
    vmem = (2 * tm * d * 4 + tm * d * 2 + 2 * 2 * d * tn * 2 + 2 * d * LANES * 2
            + 2 * 2 * tm * tn * 2 + 2 * tm * LANES * 4 + 6 * tm * tn * 4)
    return pl.pallas_call(
        functools.partial(_kv_kernel, head_dim=head_dim),
        grid=(t // tm, width // tn),
        in_specs=[
            pl.BlockSpec((tm, d), lambda m, n: (m, 0)),
            pl.BlockSpec((1, d), lambda m, n: (0, 0)),
            pl.BlockSpec((tn, d), lambda m, n: (n, 0)),
            pl.BlockSpec((d, tn), lambda m, n: (0, n + v_first)),
            pl.BlockSpec((d, LANES), lambda m, n: (0, 0)),
            pl.BlockSpec((head_dim, 1), lambda m, n: (0, 0)),
        ],
        out_specs=[
            pl.BlockSpec((tn, tm), lambda m, n: (n, m)),
            pl.BlockSpec((tm, tn), lambda m, n: (m, n)),
            pl.BlockSpec((tm, LANES), lambda m, n: (m, 0)),
        ],
        out_shape=[
            jax.ShapeDtypeStruct((width, t), BF16),
            jax.ShapeDtypeStruct((t, width), BF16),
            jax.ShapeDtypeStruct((t, LANES), F32),
        ],
        scratch_shapes=[pltpu.VMEM((tm, d), BF16)],
        compiler_params=_params(("parallel", "arbitrary"), vmem),
        name="kv_proj",
    )(h, gain.reshape(1, d), w_kt, w_kvf, w_f, k_gain.reshape(head_dim, 1))


def _forget_cumsum_kernel(f_ref, b_ref, c_ref):
    rows, seq = f_ref.shape
    lane = lax.broadcasted_iota(jnp.int32, (rows, LANES), 1)
    carry = jnp.zeros((rows, 1), F32)
    for blk in range(seq // LANES):
        cols = slice(blk * LANES, (blk + 1) * LANES)
        x = _log_sigmoid(f_ref[:, cols] + b_ref[...])
        d = 1
        while d < LANES:
            x = x + jnp.where(lane >= d, pltpu.roll(x, d, axis=1), 0.0)
            d *= 2
        x = x + carry
        c_ref[:, cols] = x
        carry = x[:, LANES - 1:LANES]


def _forget_cumsum(f_rows, b_rows):
    rows, seq = f_rows.shape
    return pl.pallas_call(
        _forget_cumsum_kernel,
        out_shape=jax.ShapeDtypeStruct((rows, seq), F32),
        name="forget_cumsum",
    )(f_rows, b_rows)


def _qg_kernel(h_ref, g_ref, wq_ref, wg_ref, qn_ref, q_ref, sg_ref, xn_ref, *, head_dim):
    _store_xn(xn_ref, h_ref, g_ref)
    xn = xn_ref[...]
    _head_rmsnorm_store(q_ref, _dot(xn, wq_ref[...]), qn_ref[...], head_dim,
                        scale=head_dim ** -0.5 * LOG2E)
    sg_ref[...] = jax.nn.sigmoid(_dot(xn, wg_ref[...])).astype(BF16)


def _qg_proj(h, gain, w_qg, q_gain, layer, tiles):
    t, d = h.shape
    width = w_qg.shape[2] // 2
    head_dim = q_gain.shape[0]
    tm, tn = min(tiles.proj_m, t), min(tiles.proj_n, width)
    nn = width // tn
    vmem = 2 * tm * d * 4 + tm * d * 2 + 2 * 2 * d * tn * 2 + 2 * 2 * tm * tn * 2 + 6 * tm * tn * 4
    return pl.pallas_call(
        functools.partial(_qg_kernel, head_dim=head_dim),
        grid=(t // tm, nn),
        in_specs=[
            pl.BlockSpec((tm, d), lambda m, n: (m, 0)),
            pl.BlockSpec((1, d), lambda m, n: (0, 0)),
            pl.BlockSpec((None, d, tn), lambda m, n: (layer, 0, n)),
            pl.BlockSpec((None, d, tn), lambda m, n: (layer, 0, n + nn)),
            pl.BlockSpec((1, head_dim), lambda m, n: (0, 0)),
        ],
        out_specs=[pl.BlockSpec((tm, tn), lambda m, n: (m, n))] * 2,
        out_shape=[jax.ShapeDtypeStruct((t, width), BF16)] * 2,
        scratch_shapes=[pltpu.VMEM((tm, d), BF16)],
        compiler_params=_params(("parallel", "arbitrary"), vmem),
        name="qg_proj",
    )(h, gain.reshape(1, d), w_qg, w_qg, q_gain.reshape(1, head_dim))


C_TERMS = 3


def _split_bf16_terms(x):
    terms = []
    for _ in range(C_TERMS):
        t = x.astype(BF16).astype(F32)
        terms.append(t)
        x = x - t
    return terms


def _attn_kernel(q_ref, kt_ref, v_ref, crow_ref, sg_ref, o_ref,
                 qa_ref, kta_ref, va_ref, m_ref, acc_ref, *, heads, head_dim):
    blk = q_ref.shape[0]
    nk = kta_ref.shape[1]
    hd = head_dim
    qi = pl.program_id(2)

    @pl.when(qi == 0)
    def _():
        sub = lax.broadcasted_iota(jnp.int32, (2 * SUBLANES, blk), 0)
        for h in range(heads):
            for kb in range(nk):
                cols = slice(kb * blk, (kb + 1) * blk)
                c_s = _split_bf16_terms(crow_ref[h, kb] * LOG2E)
                tail = jnp.where(sub < C_TERMS, 1.0, 0.0)
                for j, term in enumerate(c_s):
                    tail = jnp.where(sub == C_TERMS + j, -term, tail)
                kta_ref[h, kb, :hd, :] = kt_ref[h * hd:(h + 1) * hd, cols]
                kta_ref[h, kb, hd:hd + 2 * SUBLANES, :] = tail.astype(BF16)
                kta_ref[h, kb, hd + 2 * SUBLANES:, :] = jnp.zeros((hd - 2 * SUBLANES, blk), BF16)
            va_ref[h, :, :hd] = v_ref[:, h * hd:(h + 1) * hd]
            va_ref[h, :, hd:] = jnp.ones((v_ref.shape[0], hd), BF16)

    sub = lax.broadcasted_iota(jnp.int32, (hd, hd), 0)
    for h in range(heads):
        c_t = _split_bf16_terms(crow_ref[h, qi] * LOG2E)
        qa_ref[h, :, :hd] = q_ref[:, h * hd:(h + 1) * hd]
        for rb in range(blk // hd):
            rows = slice(rb * hd, (rb + 1) * hd)
            tail_t = jnp.where(jnp.logical_and(sub >= C_TERMS, sub < 2 * C_TERMS), 1.0, 0.0)
            for j, term in enumerate(c_t):
                tail_t = jnp.where(sub == j, term[:, rows], tail_t)
            qa_ref[h, rows, hd:] = tail_t.T.astype(BF16)

    def blocks(k0, count, diagonal):
        start = pl.multiple_of(k0 * blk, blk)
        scores = []
        for h in range(heads):
            chunks = []
            for b in range(count):
                s = _dot(qa_ref[h], kta_ref[h, k0 + b])
                if diagonal:
                    row = lax.broadcasted_iota(jnp.int32, (blk, blk), 0)
                    col = lax.broadcasted_iota(jnp.int32, (blk, blk), 1)
                    s = jnp.where(row >= col, s, -jnp.inf)
                chunks += [s[:, c * LANES:(c + 1) * LANES] for c in range(blk // LANES)]
            scores.append(chunks)
        probs = []
        for h, chunks in enumerate(scores):
            m_new = jnp.max(functools.reduce(jnp.maximum, chunks), axis=1, keepdims=True)
            if diagonal:
                m_new = jnp.broadcast_to(m_new, (blk, LANES))
                alpha = None
            else:
                m_old = m_ref[h]
                m_new = jnp.maximum(m_old, m_new)
                alpha = jnp.exp2(m_old - m_new)
            m_ref[h] = m_new
            p = jnp.concatenate([jnp.exp2(c - m_new).astype(BF16) for c in chunks], axis=1)
            probs.append((p, alpha))
        for h, (p, alpha) in enumerate(probs):
            pv = _dot(p, va_ref[h, pl.ds(start, count * blk), :])
            if diagonal:
                acc_ref[h] = pv
            else:
                acc_ref[h] = jnp.concatenate([alpha] * (2 * hd // LANES), axis=1) * acc_ref[h] + pv

    blocks(qi, 1, True)
    odd = qi % 2

    @pl.when(odd == 1)
    def _():
        blocks(0, 1, False)

    def pair(i, carry):
        blocks(2 * i + odd, 2, False)
        return carry

    lax.fori_loop(0, qi // 2, pair, 0)
    for h in range(heads):
        acc = acc_ref[h]
        o = acc[:, :hd] / acc[:, hd:]
        cols = slice(h * hd, (h + 1) * hd)
        o_ref[:, cols] = (o * sg_ref[:, cols].astype(F32)).astype(o_ref.dtype)


def _attention(q, kt, v, c, sg, batch, heads, tiles):
    t, width = q.shape
    seq = t // batch
    hd = width // heads
    assert hd == LANES
    blk = min(tiles.attn, seq)
    nq = seq // blk
    hp = min(tiles.attn_heads, heads)
    c_row = c.reshape(batch * heads, nq, 1, blk)
    qmap = lambda b, g, i: (b * nq + i, g)
    vmem = (2 * 2 * seq * hp * hd * 2 + 2 * seq * hp * 2 * hd * 2 + 2 * 3 * blk * hp * hd * 2
            + hp * blk * (2 * hd * 2 + LANES * 4 + 2 * hd * 4) + 2 * hp * seq * 4 * SUBLANES
            + 2 * hp * blk * LANES * 4 + hp * 3 * blk * blk * 4)
    return pl.pallas_call(
        functools.partial(_attn_kernel, heads=hp, head_dim=hd),
        grid=(batch, heads // hp, nq),
        in_specs=[
            pl.BlockSpec((blk, hp * hd), qmap),
            pl.BlockSpec((hp * hd, seq), lambda b, g, i: (g, b)),
            pl.BlockSpec((seq, hp * hd), lambda b, g, i: (b, g)),
            pl.BlockSpec((hp, nq, 1, blk), lambda b, g, i: (b * (heads // hp) + g, 0, 0, 0)),
            pl.BlockSpec((blk, hp * hd), qmap),
        ],
        out_specs=pl.BlockSpec((blk, hp * hd), qmap),
        out_shape=jax.ShapeDtypeStruct((t, width), BF16),
        scratch_shapes=[
            pltpu.VMEM((hp, blk, 2 * hd), BF16),
            pltpu.VMEM((hp, nq, 2 * hd, blk), BF16),
            pltpu.VMEM((hp, seq, 2 * hd), BF16),
            pltpu.VMEM((hp, blk, LANES), F32),
            pltpu.VMEM((hp, blk, 2 * hd), F32),
        ],
        compiler_params=_params(("parallel", "parallel", "arbitrary"), vmem),
        name="fox_attention",
    )(q, kt, v, c_row, sg)


def _out_proj_kernel(h_ref, a_ref, w_ref, o_ref):
    o_ref[...] = h_ref[...] + _dot(a_ref[...], w_ref[...])


def _out_proj(h, a, w, layer, tiles):
    t, d = h.shape
    kdim = a.shape[1]
    tm = min(tiles.gate_m, t)
    vmem = 4 * tm * d * 4 + 2 * tm * kdim * 2 + kdim * d * 2 + 2 * tm * d * 4
    return pl.pallas_call(
        _out_proj_kernel,
        grid=(t // tm,),
        in_specs=[
            pl.BlockSpec((tm, d), lambda m: (m, 0)),
            pl.BlockSpec((tm, kdim), lambda m: (m, 0)),
            pl.BlockSpec((None, kdim, d), lambda m: (layer, 0, 0), pipeline_mode=pl.Buffered(1)),
        ],
        out_specs=pl.BlockSpec((tm, d), lambda m: (m, 0)),
        out_shape=jax.ShapeDtypeStruct((t, d), F32),
        compiler_params=_params(("parallel",), vmem),
        name="b_out_proj",
    )(h, a, w)


def _forward(x, a_norm, a_w_in, a_v_norm, a_w_s, a_b_s, a_w_out, kv_norm, w_kvf, b_f, k_norm,
             b_norm, b_w_qg, q_norm, b_w_out, f_norm, f_w_up, f_conv_w, f_conv_b, f_w_down,
             final_norm, tiles=Tiles()):
    batch, seq, d = x.shape
    depth = f_norm.shape[0]
    n_a = a_norm.shape[0]
    heads = b_f.shape[0]
    att = heads * k_norm.shape[0]
    h = x.reshape(batch * seq, d)
    bf = lambda w: w.astype(BF16)
    a_w_in, a_w_s, a_w_out = bf(a_w_in), bf(a_w_s), bf(a_w_out)
    b_w_qg, b_w_out, w_kvf = bf(b_w_qg), bf(b_w_out), bf(w_kvf)
    ffn_dim = f_w_down.shape[1]
    cast_in_ffn = _can_cast_in_ffn(batch * seq, d, seq, ffn_dim, tiles)
    t = batch * seq
    a_width = a_w_in.shape[2] // 2
    in_grid = (t // min(tiles.proj_m, t), a_width // min(tiles.proj_n, a_width))
    first_in_mixer = (n_a > 0 and _side_cast_block(f_w_up.shape[1:], *in_grid) is not None
                      and _side_cast_block(f_w_down.shape[1:], t // min(tiles.gate_m, t), 1) is not None)
    if not first_in_mixer:
        w_up, w_down = bf(f_w_up[0]), bf(f_w_down[0])
    k = v = c = None
    for l in range(depth):
        if l == 0 and first_in_mixer:
            u, vv, w_up = _a_in(h, a_norm[l], a_w_in, l, tiles, cast=(f_w_up, 0))
            h, w_down = _a_gate(h, u, vv, a_v_norm[l], a_w_s, a_b_s[l], a_w_out, l, tiles,
                                cast=(f_w_down, 0))
        elif l < n_a:
            u, vv = _a_in(h, a_norm[l], a_w_in, l, tiles)
            h = _a_gate(h, u, vv, a_v_norm[l], a_w_s, a_b_s[l], a_w_out, l, tiles)
        else:
            j = l - n_a
            q, sg = _qg_proj(h, b_norm[j], b_w_qg, q_norm[j], j, tiles)
            og = _attention(q, k, v, c, sg, batch, heads, tiles)
            h = _out_proj(h, og, b_w_out, j, tiles)
        if l == depth - 1:
            h = _ffn(h, f_norm[l], w_up, f_conv_w, f_conv_b, w_down, l, seq, tiles,
                     final_gain=final_norm)
        elif cast_in_ffn:
            h, w_up, w_down = _ffn(h, f_norm[l], w_up, f_conv_w, f_conv_b, w_down, l, seq, tiles,
                                   next_weights=(f_w_up, f_w_down, l + 1))
        else:
            h = _ffn(h, f_norm[l], w_up, f_conv_w, f_conv_b, w_down, l, seq, tiles)
            w_up, w_down = bf(f_w_up[l + 1]), bf(f_w_down[l + 1])
        if l == n_a - 1:
            w_f = jnp.pad(w_kvf[:, 2 * att:], ((0, 0), (0, LANES - heads)))
            k, v, f = _kv_proj(h, kv_norm, w_kvf[:, :att].T, w_kvf, w_f, k_norm, tiles)
            f_rows = f[:, :heads].reshape(batch, seq, heads).transpose(0, 2, 1)
            c = _forget_cumsum(f_rows.reshape(batch * heads, seq),
                               jnp.tile(b_f, batch).reshape(batch * heads, 1))
    return h.reshape(batch, seq, d)


def kernel(x, a_norm, a_w_in, a_v_norm, a_w_s, a_b_s, a_w_out, kv_norm, w_kvf, b_f, k_norm,
           b_norm, b_w_qg, q_norm, b_w_out, f_norm, f_w_up, f_conv_w, f_conv_b, f_w_down,
           final_norm):
    return _forward(x, a_norm, a_w_in, a_v_norm, a_w_s, a_b_s, a_w_out, kv_norm, w_kvf, b_f,
                    k_norm, b_norm, b_w_qg, q_norm, b_w_out, f_norm, f_w_up, f_conv_w, f_conv_b,
                    f_w_down, final_norm)
```
